```python
import jax
import jax.numpy as jnp
from jax import lax
import numpy as np

D_MODEL = 1024
BATCH = 1
SEQ = 16384
DEPTH = 4

N_MIXERS = 2
N_LRU_LAYERS = (DEPTH + N_MIXERS - 1) // N_MIXERS
N_RWKV_LAYERS = DEPTH // N_MIXERS
D_RNN = 1280
LRU_BLOCKS = 10
LRU_BLOCK = D_RNN // LRU_BLOCKS
CONV_WIDTH = 4
LRU_C = 8.0
LRU_MIN_RAD = 0.9
LRU_MAX_RAD = 0.999
RW_HEAD = 64
RW_HEADS = D_MODEL // RW_HEAD
LORA_DECAY = 64
LORA_AAA = 64
LORA_MV = 32
LORA_GATE = 160
N_SHIFT_MIX = 6
RW_GN_EPS = 64e-5
D_FF = ((8 * D_MODEL // 3 + 255) // 256) * 256
D_PLE = 256
RMS_EPS = 1e-6

kernel_name = "hawk_rwkv7_interleaved_trunk"


def rms_norm(x, g):
    xf = x.astype(jnp.float32)
    y = xf * lax.rsqrt(jnp.mean(xf * xf, axis=-1, keepdims=True) + RMS_EPS)
    return (y * g.astype(jnp.float32)).astype(x.dtype)


def causal_depthwise_conv(x, w, b):
    K = w.shape[0]
    T = x.shape[1]
    xp = jnp.pad(x, ((0, 0), (K - 1, 0), (0, 0)))
    y = b
    for k in range(K):
        y = y + xp[:, k:k + T] * w[k]
    return y


def block_diag_linear(x, w, b):
    B, T, _ = x.shape
    xb = x.reshape(B, T, LRU_BLOCKS, LRU_BLOCK)
    y = jnp.einsum('btnc,ncd->btnd', xb, w.astype(x.dtype))
    return y.reshape(B, T, D_RNN) + b.astype(x.dtype)


def rg_lru(x, wa, ba, wx, bx, lam):
    xf = x.astype(jnp.float32)
    r = jax.nn.sigmoid(block_diag_linear(xf, wa, ba))
    i = jax.nn.sigmoid(block_diag_linear(xf, wx, bx))
    log_a = -LRU_C * r * jax.nn.softplus(-lam.astype(jnp.float32))
    a = jnp.exp(log_a)
    b = jnp.sqrt(-jnp.expm1(2.0 * log_a)) * (i * xf)

    def combine(c1, c2):
        a1, b1 = c1
        a2, b2 = c2
        return a1 * a2, a2 * b1 + b2

    _, h = lax.associative_scan(combine, (a, b), axis=1)
    return h.astype(x.dtype)


def hawk_block(x, w_in, b_in, conv_w, conv_b, wa, ba, wx, bx, lam, w_out, b_out):
    gx = x @ w_in + b_in
    gate_branch, rec_branch = jnp.split(gx, 2, axis=-1)
    gate_branch = jax.nn.gelu(gate_branch)
    rec = causal_depthwise_conv(rec_branch, conv_w, conv_b)
    rec = rg_lru(rec, wa, ba, wx, bx, lam)
    return (rec * gate_branch) @ w_out + b_out


def token_shift(x):
    return jnp.pad(x[:, :-1], ((0, 0), (1, 0), (0, 0)))


def rwkv7_scan(r, w, k, v, a_vec, b_vec):
    B, T, H, N = r.shape
    xs = tuple(jnp.moveaxis(t, 1, 0) for t in (r, w, k, v, a_vec, b_vec))

    def step(S, inp):
        r_t, w_t, k_t, v_t, a_t, b_t = inp
        sa = jnp.einsum('bhvk,bhk->bhv', S, a_t)
        S = S * w_t[:, :, None, :] + sa[..., None] * b_t[:, :, None, :] + v_t[..., None] * k_t[:, :, None, :]
        y = jnp.einsum('bhvk,bhk->bhv', S, r_t)
        return S, y

    S0 = jnp.zeros((B, H, N, N), jnp.float32)
    _, y = lax.scan(step, S0, xs)
    return jnp.moveaxis(y, 0, 1)


def rwkv7_time_mix(x, v_first, mu, w_rkv, w_o, w0, w1, w2, a0, a1, a2, g1, g2,
                   k_k, k_a, r_k, ln_w, ln_b, vres):
    B, T, D = x.shape
    f32 = jnp.float32
    xx = token_shift(x) - x
    xs = x[None] + xx[None] * mu[:, None, None, :]
    xr, xw, xk, xv, xa, xg = xs
    r, k, v = jnp.einsum('nbtd,nde->nbte', jnp.stack([xr, xk, xv]), w_rkv)
    w = -jax.nn.softplus(-(w0 + jnp.tanh(xw @ w1) @ w2)) - 0.5
    a = jax.nn.sigmoid(a0 + (xa @ a1) @ a2)
    g = jax.nn.sigmoid(xg @ g1) @ g2

    def heads(t):
        return t.reshape(B, T, RW_HEADS, RW_HEAD).astype(f32)

    kk = heads(k * k_k)
    kk = kk / jnp.maximum(jnp.sqrt(jnp.sum(kk * kk, axis=-1, keepdims=True)), 1e-12)
    k = k * (1.0 + (a - 1.0) * k_a)
    if vres is None:
        v_first = v
    else:
        v0, v1, v2 = vres
        v = v + (v_first - v) * jax.nn.sigmoid(v0 + (xv @ v1) @ v2)
    decay = jnp.exp(-jnp.exp(w.astype(f32)))
    rh, kh, vh, ah = heads(r), heads(k), heads(v), heads(a)
    y = rwkv7_scan(rh, heads(decay), kh, vh, -kk, kk * ah)
    mean = jnp.mean(y, axis=-1, keepdims=True)
    var = jnp.mean(jnp.square(y - mean), axis=-1, keepdims=True)
    yn = ((y - mean) * lax.rsqrt(var + RW_GN_EPS)).reshape(B, T, D)
    yn = yn * ln_w.astype(f32) + ln_b.astype(f32)
    bonus = jnp.sum(rh * kh * r_k.astype(f32), axis=-1, keepdims=True) * vh
    out = (yn + bonus.reshape(B, T, D)).astype(x.dtype) * g
    return out @ w_o, v_first


def swiglu(x, w_in, w_out):
    gu = x @ w_in
    gate, up = jnp.split(gu, 2, axis=-1)
    return (jax.nn.silu(gate) * up) @ w_out


def setup_inputs(seed: int = 0) -> dict:
    key = jax.random.key(seed)
    ks = iter(jax.random.split(key, 64))
    f32 = jnp.float32

    def nrm(shape, scale):
        return scale * jax.random.normal(next(ks), shape, f32)

    def gain(shape):
        return 1.0 + nrm(shape, 0.02)

    D, DR, NL, NR, L = D_MODEL, D_RNN, N_LRU_LAYERS, N_RWKV_LAYERS, DEPTH
    x = nrm((BATCH, SEQ, D), 1.0)
    p = nrm((L, BATCH, SEQ, D_PLE), 1.0)
    u = jax.random.uniform(next(ks), (NL, DR), f32, LRU_MIN_RAD, LRU_MAX_RAD)
    lru_lambda = jnp.log(u) - jnp.log1p(-u)
    rw_w0 = jax.random.uniform(next(ks), (NR, D), f32, -6.0, 1.0)
    rw_mu = jax.random.uniform(next(ks), (NR, N_SHIFT_MIX, D), f32)
    return {
        "x": x,
        "p": p,
        "norm_mix": gain((L, D)),
        "norm_ffn": gain((L, D)),
        "norm_ple": gain((L, D)),
        "norm_final": gain((D,)),
        "lru_w_in": nrm((NL, D, 2 * DR), D ** -0.5),
        "lru_b_in": nrm((NL, 2 * DR), 0.01),
        "lru_conv_w": nrm((NL, CONV_WIDTH, DR), CONV_WIDTH ** -0.5),
        "lru_conv_b": nrm((NL, DR), 0.01),
        "lru_w_gate_a": nrm((NL, LRU_BLOCKS, LRU_BLOCK, LRU_BLOCK), LRU_BLOCK ** -0.5),
        "lru_b_gate_a": nrm((NL, DR), 0.01),
        "lru_w_gate_x": nrm((NL, LRU_BLOCKS, LRU_BLOCK, LRU_BLOCK), LRU_BLOCK ** -0.5),
        "lru_b_gate_x": nrm((NL, DR), 0.01),
        "lru_lambda": lru_lambda,
        "lru_w_out": nrm((NL, DR, D), DR ** -0.5),
        "lru_b_out": nrm((NL, D), 0.01),
        "rw_mu": rw_mu,
        "rw_w_rkv": nrm((NR, 3, D, D), D ** -0.5),
        "rw_w_o": nrm((NR, D, D), D ** -0.5),
        "rw_w0": rw_w0,
        "rw_w1": nrm((NR, D, LORA_DECAY), D ** -0.5),
        "rw_w2": nrm((NR, LORA_DECAY, D), 0.1 * LORA_DECAY ** -0.5),
        "rw_a0": nrm((NR, D), 0.5),
        "rw_a1": nrm((NR, D, LORA_AAA), D ** -0.5),
        "rw_a2": nrm((NR, LORA_AAA, D), 0.1 * LORA_AAA ** -0.5),
        "rw_v0": 1.0 + nrm((NR - 1, D), 0.1),
        "rw_v1": nrm((NR - 1, D, LORA_MV), D ** -0.5),
        "rw_v2": nrm((NR - 1, LORA_MV, D), 0.1 * LORA_MV ** -0.5),
        "rw_g1": nrm((NR, D, LORA_GATE), D ** -0.5),
        "rw_g2": nrm((NR, LORA_GATE, D), LORA_GATE ** -0.5),
        "rw_k_k": 0.85 + nrm((NR, D), 0.05),
        "rw_k_a": 1.0 + nrm((NR, D), 0.05),
        "rw_r_k": nrm((NR, RW_HEADS, RW_HEAD), 0.1),
        "rw_ln_w": gain((NR, D)),
        "rw_ln_b": nrm((NR, D), 0.01),
        "ffn_w_in": nrm((L, D, 2 * D_FF), D ** -0.5),
        "ffn_w_out": nrm((L, D_FF, D), D_FF ** -0.5),
        "ple_w_proj": nrm((L, D_PLE, D), D_PLE ** -0.5),
        "ple_w_gate": nrm((L, D, D), D ** -0.5),
    }


def reference(x, p, norm_mix, norm_ffn, norm_ple, norm_final,
              lru_w_in, lru_b_in, lru_conv_w, lru_conv_b, lru_w_gate_a, lru_b_gate_a,
              lru_w_gate_x, lru_b_gate_x, lru_lambda, lru_w_out, lru_b_out,
              rw_mu, rw_w_rkv, rw_w_o, rw_w0, rw_w1, rw_w2, rw_a0, rw_a1, rw_a2,
              rw_v0, rw_v1, rw_v2, rw_g1, rw_g2, rw_k_k, rw_k_a, rw_r_k, rw_ln_w, rw_ln_b,
              ffn_w_in, ffn_w_out, ple_w_proj, ple_w_gate):
    h = x
    v_first = None
    for i in range(DEPTH):
        j = i // N_MIXERS
        xn = rms_norm(h, norm_mix[i])
        if i % N_MIXERS == 0:
            mix = hawk_block(xn, lru_w_in[j], lru_b_in[j], lru_conv_w[j], lru_conv_b[j],
                             lru_w_gate_a[j], lru_b_gate_a[j], lru_w_gate_x[j], lru_b_gate_x[j],
                             lru_lambda[j], lru_w_out[j], lru_b_out[j])
        else:
            vres = None if j == 0 else (rw_v0[j - 1], rw_v1[j - 1], rw_v2[j - 1])
            mix, v_first = rwkv7_time_mix(xn, v_first, rw_mu[j], rw_w_rkv[j], rw_w_o[j],
                                          rw_w0[j], rw_w1[j], rw_w2[j], rw_a0[j], rw_a1[j], rw_a2[j],
                                          rw_g1[j], rw_g2[j], rw_k_k[j], rw_k_a[j], rw_r_k[j],
                                          rw_ln_w[j], rw_ln_b[j], vres)
        h = h + mix
        h = h + swiglu(rms_norm(h, norm_ffn[i]), ffn_w_in[i], ffn_w_out[i])
        ple = (p[i] @ ple_w_proj[i]) * jax.nn.sigmoid(rms_norm(h, norm_ple[i]) @ ple_w_gate[i])
        h = h + ple
    return rms_norm(h, norm_final)
```

```python
import functools

import numpy as np
import jax
import jax.numpy as jnp
from jax import lax
from jax.experimental import pallas as pl
from jax.experimental.pallas import tpu as pltpu

F32 = jnp.float32
BF16 = jnp.bfloat16

D_MODEL = 1024
DEPTH = 4
N_MIXERS = 2
D_RNN = 1280
LRU_BLOCKS = 10
LRU_BLOCK = D_RNN // LRU_BLOCKS
CONV_WIDTH = 4
LRU_C = 8.0
RW_HEAD = 64
RW_HEADS = D_MODEL // RW_HEAD
RW_GN_EPS = 64e-5
D_FF = 2816
D_PLE = 256
RMS_EPS = 1e-6

LANES = 128
SUBLANES = 8
TB_FFN = 512
TB_HAWK = 256
TB_RW = 256
TB_SCAN = 64
VMEM_LIMIT = 56 * 1024 * 1024


def _cparams():
    return pltpu.CompilerParams(dimension_semantics=("arbitrary",), vmem_limit_bytes=VMEM_LIMIT)


def _const_spec(shape):
    nd = len(shape)
    return pl.BlockSpec(shape, lambda i: (0,) * nd, pipeline_mode=pl.Buffered(1))


def _row_spec(tb, width):
    return pl.BlockSpec((tb, width), lambda i: (i, 0))


def _rms(x, g):
    return x * lax.rsqrt(jnp.mean(x * x, axis=-1, keepdims=True) + RMS_EPS) * g


def _mm(x, w):
    return jnp.dot(x.astype(BF16), w, preferred_element_type=F32)


def _split3(x):
    hi = x.astype(BF16)
    r1 = x - hi.astype(F32)
    mid = r1.astype(BF16)
    lo = (r1 - mid.astype(F32)).astype(BF16)
    return hi, mid, lo


def _mm_exact01(x, m):
    hi, mid, lo = _split3(x)
    acc = jnp.dot(hi, m, preferred_element_type=F32)
    acc = acc + jnp.dot(mid, m, preferred_element_type=F32)
    return acc + jnp.dot(lo, m, preferred_element_type=F32)


def _lane_tiles(x):
    return [x[:, LANES * j:LANES * (j + 1)] for j in range(x.shape[1] // LANES)]


def _tile_sum(x):
    tiles = _lane_tiles(x)
    acc = tiles[0]
    for t in tiles[1:]:
        acc = acc + t
    return acc


def _ffn_ple_kernel(h_ref, p_ref, gf_ref, gp_ref, gl_ref, win_ref, wout_ref, wproj_ref, wgate_ref, o_ref, *, final):
    h = h_ref[...]
    gu = _mm(_rms(h, gf_ref[...]), win_ref[...])
    act = jax.nn.silu(gu[:, :D_FF]) * gu[:, D_FF:]
    h = h + _mm(act, wout_ref[...])
    gate = jax.nn.sigmoid(_mm(_rms(h, gp_ref[...]), wgate_ref[...]))
    h = h + _mm(p_ref[...], wproj_ref[...]) * gate
    if final:
        h = _rms(h, gl_ref[...])
    o_ref[...] = h


def _ffn_ple(h, p3, layer, g_ffn, g_ple, g_final, w_in, w_out, w_proj, w_gate, final):
    t = h.shape[0]
    tb = TB_FFN
    return pl.pallas_call(
        functools.partial(_ffn_ple_kernel, final=final),
        grid=(t // tb,),
        in_specs=[
            _row_spec(tb, D_MODEL),
            pl.BlockSpec((None, tb, D_PLE), lambda i: (layer, i, 0)),
            _const_spec((1, D_MODEL)), _const_spec((1, D_MODEL)), _const_spec((1, D_MODEL)),
            _const_spec((D_MODEL, 2 * D_FF)), _const_spec((D_FF, D_MODEL)),
            _const_spec((D_PLE, D_MODEL)), _const_spec((D_MODEL, D_MODEL)),
        ],
        out_specs=_row_spec(tb, D_MODEL),
        out_shape=jax.ShapeDtypeStruct((t, D_MODEL), F32),
        compiler_params=_cparams(),
        name="ffn_ple",
    )(h, p3, g_ffn, g_ple, g_final, w_in, w_out, w_proj, w_gate)


def _hawk_kernel(h_ref, g_ref, win_ref, bin_ref, cw_ref, cb_ref, wa_ref, ba_ref, wx_ref, bx_ref,
                 lam_ref, wout_ref, bout_ref, o_ref, rec_buf, state_ref, *, tb):
    @pl.when(pl.program_id(0) == 0)
    def _():
        rec_buf[0:SUBLANES, :] = jnp.zeros((SUBLANES, D_RNN), F32)
        state_ref[...] = jnp.zeros_like(state_ref)

    h = h_ref[...]
    gx = _mm(_rms(h, g_ref[...]), win_ref[...]) + bin_ref[...]
    gate = jax.nn.gelu(gx[:, :D_RNN])
    rec_buf[SUBLANES:SUBLANES + tb, :] = gx[:, D_RNN:]

    conv = cb_ref[...]
    for k in range(CONV_WIDTH):
        off = SUBLANES - (CONV_WIDTH - 1) + k
        conv = conv + rec_buf[off:off + tb, :] * cw_ref[k:k + 1, :]
    rec_buf[0:SUBLANES, :] = rec_buf[tb:tb + SUBLANES, :]

    xb = conv.astype(BF16)
    ra, rx = [], []
    for n in range(LRU_BLOCKS):
        xs = xb[:, LRU_BLOCK * n:LRU_BLOCK * (n + 1)]
        ra.append(jnp.dot(xs, wa_ref[n], preferred_element_type=F32))
        rx.append(jnp.dot(xs, wx_ref[n], preferred_element_type=F32))
    r = jax.nn.sigmoid(jnp.concatenate(ra, axis=1) + ba_ref[...])
    i = jax.nn.sigmoid(jnp.concatenate(rx, axis=1) + bx_ref[...])
    log_a = -LRU_C * r * jax.nn.softplus(-lam_ref[...])
    a = jnp.exp(log_a)
    th = jnp.tanh(log_a)
    b = jnp.sqrt(-2.0 * th / (1.0 - th)) * (i * conv)

    row = lax.broadcasted_iota(jnp.int32, (tb, D_RNN), 0)
    s = 1
    while s < tb:
        keep = row >= s
        a_s = jnp.where(keep, pltpu.roll(a, s, 0), 1.0)
        b_s = jnp.where(keep, pltpu.roll(b, s, 0), 0.0)
        b = a * b_s + b
        a = a * a_s
        s *= 2
    hs = a * state_ref[...] + b
    state_ref[...] = hs[tb - 1:tb, :]

    o_ref[...] = h + _mm(hs * gate, wout_ref[...]) + bout_ref[...]


def _hawk(h, g, w_in, b_in, conv_w, conv_b, wa, ba, wx, bx, lam, w_out, b_out):
    t = h.shape[0]
    tb = TB_HAWK
    return pl.pallas_call(
        functools.partial(_hawk_kernel, tb=tb),
        grid=(t // tb,),
        in_specs=[
            _row_spec(tb, D_MODEL), _const_spec((1, D_MODEL)),
            _const_spec((D_MODEL, 2 * D_RNN)), _const_spec((1, 2 * D_RNN)),
            _const_spec((CONV_WIDTH, D_RNN)), _const_spec((1, D_RNN)),
            _const_spec((LRU_BLOCKS, LRU_BLOCK, LRU_BLOCK)), _const_spec((1, D_RNN)),
            _const_spec((LRU_BLOCKS, LRU_BLOCK, LRU_BLOCK)), _const_spec((1, D_RNN)),
            _const_spec((1, D_RNN)), _const_spec((D_RNN, D_MODEL)), _const_spec((1, D_MODEL)),
        ],
        out_specs=_row_spec(tb, D_MODEL),
        out_shape=jax.ShapeDtypeStruct((t, D_MODEL), F32),
        scratch_shapes=[pltpu.VMEM((tb + SUBLANES, D_RNN), F32), pltpu.VMEM((1, D_RNN), F32)],
        compiler_params=_cparams(),
        name="hawk",
    )(h, g, w_in, b_in, conv_w, conv_b, wa, ba, wx, bx, lam, w_out, b_out)


def _rwkv_pre_kernel(*refs, tb, has_vres):
    if has_vres:
        (h_ref, g_ref, mu_ref, wr_ref, wk_ref, wv_ref, w0_ref, w1_ref, w2_ref, a0_ref, a1_ref, a2_ref,
         g1_ref, g2_ref, kk_ref, ka_ref, rk_ref, mkk_ref, mkv_ref, vf_ref, v0_ref, v1_ref, v2_ref,
         r_o, w_o, k_o, a_o, b_o, v_o, g_o, bon_o, x_buf) = refs
    else:
        (h_ref, g_ref, mu_ref, wr_ref, wk_ref, wv_ref, w0_ref, w1_ref, w2_ref, a0_ref, a1_ref, a2_ref,
         g1_ref, g2_ref, kk_ref, ka_ref, rk_ref, mkk_ref, mkv_ref,
         r_o, w_o, k_o, a_o, b_o, v_o, g_o, bon_o, x_buf) = refs

    @pl.when(pl.program_id(0) == 0)
    def _():
        x_buf[0:SUBLANES, :] = jnp.zeros((SUBLANES, D_MODEL), F32)

    x = _rms(h_ref[...], g_ref[...])
    x_buf[SUBLANES:SUBLANES + tb, :] = x
    xx = x_buf[SUBLANES - 1:SUBLANES - 1 + tb, :] - x
    x_buf[0:SUBLANES, :] = x_buf[tb:tb + SUBLANES, :]

    def mix(n):
        return x + xx * mu_ref[n:n + 1, :]

    r = _mm(mix(0), wr_ref[...])
    k = _mm(mix(2), wk_ref[...])
    xv = mix(3)
    v = _mm(xv, wv_ref[...])
    w = -jax.nn.softplus(-(w0_ref[...] + _mm(jnp.tanh(_mm(mix(1), w1_ref[...])), w2_ref[...]))) - 0.5
    a = jax.nn.sigmoid(a0_ref[...] + _mm(_mm(mix(4), a1_ref[...]), a2_ref[...]))
    g_o[...] = _mm(jax.nn.sigmoid(_mm(mix(5), g1_ref[...])), g2_ref[...])

    kk = k * kk_ref[...]
    ss = _mm_exact01(_tile_sum(kk * kk), mkk_ref[...])
    inv = 1.0 / jnp.maximum(jnp.sqrt(ss), 1e-12)
    kk = jnp.concatenate([t * inv for t in _lane_tiles(kk)], axis=1)
    k = k * (1.0 + (a - 1.0) * ka_ref[...])
    if has_vres:
        v = v + (vf_ref[...] - v) * jax.nn.sigmoid(v0_ref[...] + _mm(_mm(xv, v1_ref[...]), v2_ref[...]))

    r_o[...] = r
    w_o[...] = jnp.exp(-jnp.exp(w))
    k_o[...] = k
    a_o[...] = -kk
    b_o[...] = kk * a
    v_o[...] = v
    bon_o[...] = _mm_exact01(_tile_sum(r * k * rk_ref[...]), mkv_ref[...])


def _rwkv_pre(h, g, mu, wr, wk, wv, w0, w1, w2, a0, a1, a2, g1, g2, k_k, k_a, r_k, mkk, mkv, vres):
    t = h.shape[0]
    tb = TB_RW
    lw, la, lg = w1.shape[1], a1.shape[1], g1.shape[1]
    in_specs = [
        _row_spec(tb, D_MODEL), _const_spec((1, D_MODEL)), _const_spec((6, D_MODEL)),
        _const_spec((D_MODEL, D_MODEL)), _const_spec((D_MODEL, D_MODEL)), _const_spec((D_MODEL, D_MODEL)),
        _const_spec((1, D_MODEL)), _const_spec((D_MODEL, lw)), _const_spec((lw, D_MODEL)),
        _const_spec((1, D_MODEL)), _const_spec((D_MODEL, la)), _const_spec((la, D_MODEL)),
        _const_spec((D_MODEL, lg)), _const_spec((lg, D_MODEL)),
        _const_spec((1, D_MODEL)), _const_spec((1, D_MODEL)), _const_spec((1, D_MODEL)),
        _const_spec((LANES, LANES)), _const_spec((LANES, LANES)),
    ]
    args = [h, g, mu, wr, wk, wv, w0, w1, w2, a0, a1, a2, g1, g2, k_k, k_a, r_k, mkk, mkv]
    if vres is not None:
        v_first, v0, v1, v2 = vres
        lv = v1.shape[1]
        in_specs += [_row_spec(tb, D_MODEL), _const_spec((1, D_MODEL)),
                     _const_spec((D_MODEL, lv)), _const_spec((lv, D_MODEL))]
        args += [v_first, v0, v1, v2]
    wide = jax.ShapeDtypeStruct((t, D_MODEL), F32)
    return pl.pallas_call(
        functools.partial(_rwkv_pre_kernel, tb=tb, has_vres=vres is not None),
        grid=(t // tb,),
        in_specs=in_specs,
        out_specs=[_row_spec(tb, D_MODEL)] * 7 + [_row_spec(tb, LANES)],
        out_shape=[wide] * 7 + [jax.ShapeDtypeStruct((t, LANES), F32)],
        scratch_shapes=[pltpu.VMEM((tb + SUBLANES, D_MODEL), F32)],
        compiler_params=_cparams(),
        name="rwkv_pre",
    )(*args)


def _scan_kernel(r_ref, w_ref, k_ref, a_ref, b_ref, v_ref, rm_ref, y_ref,
                 s_ref, er_ref, ew_ref, ek_ref, ea_ref, eb_ref, *, tb):
    @pl.when(pl.program_id(0) == 0)
    def _():
        s_ref[...] = jnp.zeros_like(s_ref)

    rm = rm_ref[...]
    n_tiles = D_MODEL // LANES
    for src, dst in ((r_ref, er_ref), (w_ref, ew_ref), (k_ref, ek_ref), (a_ref, ea_ref), (b_ref, eb_ref)):
        parts = [_split3(src[:, LANES * j:LANES * (j + 1)]) for j in range(n_tiles)]
        acc = None
        for p in range(3):
            lhs = jnp.concatenate([parts[j][p] for j in range(n_tiles)], axis=0)
            d = jnp.dot(lhs, rm, preferred_element_type=F32)
            acc = d if acc is None else acc + d
        for j in range(n_tiles):
            for kl in range(SUBLANES):
                dst[SUBLANES * j + kl] = acc[j * tb:(j + 1) * tb, LANES * kl:LANES * (kl + 1)]

    def bc(ref, k, t):
        return jnp.broadcast_to(ref[k, pl.ds(t, 1), :], (SUBLANES, LANES))

    def tree(parts):
        return (parts[0] + parts[1]) + (parts[2] + parts[3])

    def acc_into(parts, k, term):
        parts[k % 4] = term if parts[k % 4] is None else parts[k % 4] + term

    def group(i, carry):
        t0 = i * SUBLANES
        up = [None] * 4
        for k in range(RW_HEAD):
            acc_into(up, k, s_ref[k] * bc(ea_ref, k, t0))
        u = tree(up)
        for j in range(SUBLANES):
            t = t0 + j
            val = v_ref[t]
            yp = [None] * 4
            up = [None] * 4
            for k in range(RW_HEAD):
                sn = s_ref[k] * bc(ew_ref, k, t) + u * bc(eb_ref, k, t) + val * bc(ek_ref, k, t)
                s_ref[k] = sn
                acc_into(yp, k, sn * bc(er_ref, k, t))
                if j + 1 < SUBLANES:
                    acc_into(up, k, sn * bc(ea_ref, k, t + 1))
            y_ref[t] = tree(yp)
            if j + 1 < SUBLANES:
                u = tree(up)
        return carry

    lax.fori_loop(0, tb // SUBLANES, group, 0)


def _rwkv_scan(r, w, k, a, b, v3, rm):
    t = r.shape[0]
    tb = TB_SCAN
    kspec = _row_spec(tb, D_MODEL)
    vspec = pl.BlockSpec((tb, SUBLANES, LANES), lambda i: (i, 0, 0))
    e_scr = pltpu.VMEM((RW_HEAD, tb, LANES), F32)
    return pl.pallas_call(
        functools.partial(_scan_kernel, tb=tb),
        grid=(t // tb,),
        in_specs=[kspec] * 5 + [vspec, _const_spec((LANES, D_MODEL))],
        out_specs=vspec,
        out_shape=jax.ShapeDtypeStruct((t, SUBLANES, LANES), F32),
        scratch_shapes=[pltpu.VMEM((RW_HEAD, SUBLANES, LANES), F32)] + [e_scr] * 5,
        compiler_params=_cparams(),
        name="rwkv_scan",
    )(r, w, k, a, b, v3, rm)


def _rwkv_post_kernel(h_ref, y_ref, v_ref, g_ref, bon_ref, lnw_ref, lnb_ref, mvv_ref, wo_ref, o_ref):
    y = y_ref[...]
    inv_n = 1.0 / RW_HEAD
    mean = _mm_exact01(_tile_sum(y), mvv_ref[...]) * inv_n
    cen = [t - mean for t in _lane_tiles(y)]
    sq = cen[0] * cen[0]
    for c in cen[1:]:
        sq = sq + c * c
    rstd = lax.rsqrt(_mm_exact01(sq, mvv_ref[...]) * inv_n + RW_GN_EPS)
    yn = jnp.concatenate([c * rstd for c in cen], axis=1) * lnw_ref[...] + lnb_ref[...]
    bon = bon_ref[...]
    bonus = jnp.concatenate([bon * t for t in _lane_tiles(v_ref[...])], axis=1)
    out = (yn + bonus) * g_ref[...]
    o_ref[...] = h_ref[...] + _mm(out, wo_ref[...])


def _rwkv_post(h, y, v, g, bon, ln_w, ln_b, mvv, w_o):
    t = h.shape[0]
    tb = TB_RW
    return pl.pallas_call(
        _rwkv_post_kernel,
        grid=(t // tb,),
        in_specs=[_row_spec(tb, D_MODEL)] * 4 + [_row_spec(tb, LANES), _const_spec((1, D_MODEL)),
                  _const_spec((1, D_MODEL)), _const_spec((LANES, LANES)), _const_spec((D_MODEL, D_MODEL))],
        out_specs=_row_spec(tb, D_MODEL),
        out_shape=jax.ShapeDtypeStruct((t, D_MODEL), F32),
        compiler_params=_cparams(),
        name="rwkv_post",
    )(h, y, v, g, bon, ln_w, ln_b, mvv, w_o)


def _k_perm():
    c = np.arange(D_MODEL)
    return (c % RW_HEADS) * RW_HEAD + (c // RW_HEADS)


def _v_perm():
    c = np.arange(D_MODEL)
    vhi, head, vlo = c // LANES, (c % LANES) // SUBLANES, c % SUBLANES
    return head * RW_HEAD + vhi * SUBLANES + vlo


def _segment_matrices():
    i = np.arange(LANES)[:, None]
    c = np.arange(LANES)[None, :]
    mkk = (i % RW_HEADS) == (c % RW_HEADS)
    mkv = (i % RW_HEADS) == (c // SUBLANES)
    mvv = (i // SUBLANES) == (c // SUBLANES)
    ce = np.arange(D_MODEL)[None, :]
    rm = ((i // RW_HEADS) == (ce // LANES)) & ((i % RW_HEADS) == ((ce % LANES) // SUBLANES))
    return tuple(jnp.asarray(m, BF16) for m in (mkk, mkv, mvv, rm))


def kernel(x, p, norm_mix, norm_ffn, norm_ple, norm_final, lru_w_in, lru_b_in, lru_conv_w, lru_conv_b, lru_w_gate_a, lru_b_gate_a, lru_w_gate_x, lru_b_gate_x, lru_lambda, lru_w_out, lru_b_out, rw_mu, rw_w_rkv, rw_w_o, rw_w0, rw_w1, rw_w2, rw_a0, rw_a1, rw_a2, rw_v0, rw_v1, rw_v2, rw_g1, rw_g2, rw_k_k, rw_k_a, rw_r_k, rw_ln_w, rw_ln_b, ffn_w_in, ffn_w_out, ple_w_proj, ple_w_gate):
    batch, seq, d = x.shape
    assert batch == 1 and d == D_MODEL and seq % TB_FFN == 0
    h = x.reshape(seq, d)
    p3 = p.reshape(DEPTH, seq, D_PLE)
    kp, vp = _k_perm(), _v_perm()
    mkk, mkv, mvv, rm = _segment_matrices()
    row = lambda a: a.reshape(1, -1)
    bf = lambda a: a.astype(BF16)
    g_final = row(norm_final)
    v_first = None
    for i in range(DEPTH):
        j = i // N_MIXERS
        g_mix = row(norm_mix[i])
        if i % N_MIXERS == 0:
            h = _hawk(h, g_mix, bf(lru_w_in[j]), row(lru_b_in[j]), lru_conv_w[j], row(lru_conv_b[j]),
                      bf(lru_w_gate_a[j]), row(lru_b_gate_a[j]), bf(lru_w_gate_x[j]), row(lru_b_gate_x[j]),
                      row(lru_lambda[j]), bf(lru_w_out[j]), row(lru_b_out[j]))
        else:
            vres = None if j == 0 else (v_first, row(rw_v0[j - 1][vp]), bf(rw_v1[j - 1]), bf(rw_v2[j - 1][:, vp]))
            r, w, k, a, b, v, g, bon = _rwkv_pre(
                h, g_mix, rw_mu[j], bf(rw_w_rkv[j, 0][:, kp]), bf(rw_w_rkv[j, 1][:, kp]), bf(rw_w_rkv[j, 2][:, vp]),
                row(rw_w0[j][kp]), bf(rw_w1[j]), bf(rw_w2[j][:, kp]),
                row(rw_a0[j][kp]), bf(rw_a1[j]), bf(rw_a2[j][:, kp]),
                bf(rw_g1[j]), bf(rw_g2[j][:, vp]),
                row(rw_k_k[j][kp]), row(rw_k_a[j][kp]), row(rw_r_k[j].reshape(-1)[kp]), mkk, mkv, vres)
            if j == 0:
                v_first = v
            y = _rwkv_scan(r, w, k, a, b, v.reshape(seq, SUBLANES, LANES), rm)
            h = _rwkv_post(h, y.reshape(seq, d), v, g, bon, row(rw_ln_w[j][vp]), row(rw_ln_b[j][vp]), mvv,
                           bf(rw_w_o[j][vp, :]))
        h = _ffn_ple(h, p3, i, row(norm_ffn[i]), row(norm_ple[i]), g_final, bf(ffn_w_in[i]), bf(ffn_w_out[i]),
                     bf(ple_w_proj[i]), bf(ple_w_gate[i]), final=(i == DEPTH - 1))
    return h.reshape(batch, seq, d)
```

```python
import functools

import numpy as np
import jax
import jax.numpy as jnp
from jax import lax
from jax.experimental import pallas as pl
from jax.experimental.pallas import tpu as pltpu

F32 = jnp.float32
BF16 = jnp.bfloat16

D_MODEL = 1024
DEPTH = 4
N_MIXERS = 2
D_RNN = 1280
LRU_BLOCKS = 10
LRU_BLOCK = D_RNN // LRU_BLOCKS
CONV_WIDTH = 4
LRU_C = 8.0
RW_HEAD = 64
RW_HEADS = D_MODEL // RW_HEAD
RW_GN_EPS = 64e-5
D_FF = 2816
D_PLE = 256
RMS_EPS = 1e-6

LANES = 128
SUBLANES = 8
MXU_N = 256
TB_FFN = 512
TB_HAWK = 256
TB_RW = 256
TB_SCAN = 128
VMEM_LIMIT = 56 * 1024 * 1024

N_TILES = D_MODEL // LANES
N_COEF = 5
N_SPLIT = 3
N_RB = N_COEF * N_TILES
HALF_RB = N_RB // 2
N_ACC = 4


def _cparams():
    return pltpu.CompilerParams(dimension_semantics=("arbitrary",), vmem_limit_bytes=VMEM_LIMIT)


def _const_spec(shape):
    nd = len(shape)
    return pl.BlockSpec(shape, lambda i: (0,) * nd, pipeline_mode=pl.Buffered(1))


def _row_spec(tb, width):
    return pl.BlockSpec((tb, width), lambda i: (i, 0))


def _rms(x, g):
    return x * lax.rsqrt(jnp.mean(x * x, axis=-1, keepdims=True) + RMS_EPS) * g


def _mm(x, w):
    return jnp.dot(x.astype(BF16), w, preferred_element_type=F32)


def _split3(x):
    hi = x.astype(BF16)
    r1 = x - hi.astype(F32)
    mid = r1.astype(BF16)
    lo = (r1 - mid.astype(F32)).astype(BF16)
    return hi, mid, lo


def _mm_exact01(x, m):
    hi, mid, lo = _split3(x)
    acc = jnp.dot(hi, m, preferred_element_type=F32)
    acc = acc + jnp.dot(mid, m, preferred_element_type=F32)
    return acc + jnp.dot(lo, m, preferred_element_type=F32)


def _lane_tiles(x):
    return [x[:, LANES * j:LANES * (j + 1)] for j in range(x.shape[1] // LANES)]


def _tile_sum(x):
    tiles = _lane_tiles(x)
    acc = tiles[0]
    for t in tiles[1:]:
        acc = acc + t
    return acc


def _tree_sum(parts):
    while len(parts) > 1:
        parts = [parts[i] + parts[i + 1] if i + 1 < len(parts) else parts[i] for i in range(0, len(parts), 2)]
    return parts[0]


def _ffn_ple_kernel(h_ref, p_ref, gf_ref, gp_ref, gl_ref, win_ref, wout_ref, wproj_ref, wgate_ref, o_ref, *, final):
    h = h_ref[...]
    gu = _mm(_rms(h, gf_ref[...]), win_ref[...])
    act = jax.nn.silu(gu[:, :D_FF]) * gu[:, D_FF:]
    h = h + _mm(act, wout_ref[...])
    gate = jax.nn.sigmoid(_mm(_rms(h, gp_ref[...]), wgate_ref[...]))
    h = h + _mm(p_ref[...], wproj_ref[...]) * gate
    if final:
        h = _rms(h, gl_ref[...])
    o_ref[...] = h


def _ffn_ple(h, p3, layer, g_ffn, g_ple, g_final, w_in, w_out, w_proj, w_gate, final):
    t = h.shape[0]
    tb = TB_FFN
    return pl.pallas_call(
        functools.partial(_ffn_ple_kernel, final=final),
        grid=(t // tb,),
        in_specs=[
            _row_spec(tb, D_MODEL),
            pl.BlockSpec((None, tb, D_PLE), lambda i: (layer, i, 0)),
            _const_spec((1, D_MODEL)), _const_spec((1, D_MODEL)), _const_spec((1, D_MODEL)),
            _const_spec((D_MODEL, 2 * D_FF)), _const_spec((D_FF, D_MODEL)),
            _const_spec((D_PLE, D_MODEL)), _const_spec((D_MODEL, D_MODEL)),
        ],
        out_specs=_row_spec(tb, D_MODEL),
        out_shape=jax.ShapeDtypeStruct((t, D_MODEL), F32),
        compiler_params=_cparams(),
        name="ffn_ple",
    )(h, p3, g_ffn, g_ple, g_final, w_in, w_out, w_proj, w_gate)


def _hawk_kernel(h_ref, g_ref, win_ref, bin_ref, cw_ref, cb_ref, wa_ref, ba_ref, wx_ref, bx_ref,
                 lam_ref, wout_ref, bout_ref, o_ref, rec_buf, state_ref, *, tb):
    @pl.when(pl.program_id(0) == 0)
    def _():
        rec_buf[0:SUBLANES, :] = jnp.zeros((SUBLANES, D_RNN), F32)
        state_ref[...] = jnp.zeros_like(state_ref)

    h = h_ref[...]
    gx = _mm(_rms(h, g_ref[...]), win_ref[...]) + bin_ref[...]
    gate = jax.nn.gelu(gx[:, :D_RNN])
    rec_buf[SUBLANES:SUBLANES + tb, :] = gx[:, D_RNN:]

    conv = cb_ref[...]
    for k in range(CONV_WIDTH):
        off = SUBLANES - (CONV_WIDTH - 1) + k
        conv = conv + rec_buf[off:off + tb, :] * cw_ref[k:k + 1, :]
    rec_buf[0:SUBLANES, :] = rec_buf[tb:tb + SUBLANES, :]

    xb = conv.astype(BF16)
    ra, rx = [], []
    for n in range(LRU_BLOCKS):
        xs = xb[:, LRU_BLOCK * n:LRU_BLOCK * (n + 1)]
        ra.append(jnp.dot(xs, wa_ref[n], preferred_element_type=F32))
        rx.append(jnp.dot(xs, wx_ref[n], preferred_element_type=F32))
    r = jax.nn.sigmoid(jnp.concatenate(ra, axis=1) + ba_ref[...])
    i = jax.nn.sigmoid(jnp.concatenate(rx, axis=1) + bx_ref[...])
    log_a = -LRU_C * r * jax.nn.softplus(-lam_ref[...])
    a = jnp.exp(log_a)
    th = jnp.tanh(log_a)
    b = jnp.sqrt(-2.0 * th / (1.0 - th)) * (i * conv)

    n_groups = tb // SUBLANES
    a = a.reshape(n_groups, SUBLANES, D_RNN)
    b = b.reshape(n_groups, SUBLANES, D_RNN)
    row = lax.broadcasted_iota(jnp.int32, (n_groups, SUBLANES, D_RNN), 1)
    s = 1
    while s < SUBLANES:
        keep = row >= s
        a_s = jnp.where(keep, pltpu.roll(a, s, 1), 1.0)
        b_s = jnp.where(keep, pltpu.roll(b, s, 1), 0.0)
        b = a * b_s + b
        a = a * a_s
        s *= 2
    carry = state_ref[...]
    groups = []
    for n in range(n_groups):
        hg = a[n] * carry + b[n]
        groups.append(hg)
        carry = hg[SUBLANES - 1:SUBLANES, :]
    state_ref[...] = carry
    hs = jnp.concatenate(groups, axis=0)

    o_ref[...] = h + _mm(hs * gate, wout_ref[...]) + bout_ref[...]


def _hawk(h, g, w_in, b_in, conv_w, conv_b, wa, ba, wx, bx, lam, w_out, b_out):
    t = h.shape[0]
    tb = TB_HAWK
    return pl.pallas_call(
        functools.partial(_hawk_kernel, tb=tb),
        grid=(t // tb,),
        in_specs=[
            _row_spec(tb, D_MODEL), _const_spec((1, D_MODEL)),
            _const_spec((D_MODEL, 2 * D_RNN)), _const_spec((1, 2 * D_RNN)),
            _const_spec((CONV_WIDTH, D_RNN)), _const_spec((1, D_RNN)),
            _const_spec((LRU_BLOCKS, LRU_BLOCK, LRU_BLOCK)), _const_spec((1, D_RNN)),
            _const_spec((LRU_BLOCKS, LRU_BLOCK, LRU_BLOCK)), _const_spec((1, D_RNN)),
            _const_spec((1, D_RNN)), _const_spec((D_RNN, D_MODEL)), _const_spec((1, D_MODEL)),
        ],
        out_specs=_row_spec(tb, D_MODEL),
        out_shape=jax.ShapeDtypeStruct((t, D_MODEL), F32),
        scratch_shapes=[pltpu.VMEM((tb + SUBLANES, D_RNN), F32), pltpu.VMEM((1, D_RNN), F32)],
        compiler_params=_cparams(),
        name="hawk",
    )(h, g, w_in, b_in, conv_w, conv_b, wa, ba, wx, bx, lam, w_out, b_out)


def _rwkv_pre_kernel(*refs, tb, has_vres):
    if has_vres:
        (h_ref, g_ref, mu_ref, wr_ref, wk_ref, wv_ref, w0_ref, w1_ref, w2_ref, a0_ref, a1_ref, a2_ref,
         g1_ref, g2_ref, kk_ref, ka_ref, rk_ref, mkk_ref, mkv_ref, vf_ref, v0_ref, v1_ref, v2_ref,
         r_o, w_o, k_o, a_o, b_o, v_o, g_o, bon_o, x_buf) = refs
    else:
        (h_ref, g_ref, mu_ref, wr_ref, wk_ref, wv_ref, w0_ref, w1_ref, w2_ref, a0_ref, a1_ref, a2_ref,
         g1_ref, g2_ref, kk_ref, ka_ref, rk_ref, mkk_ref, mkv_ref,
         r_o, w_o, k_o, a_o, b_o, v_o, g_o, bon_o, x_buf) = refs

    @pl.when(pl.program_id(0) == 0)
    def _():
        x_buf[0:SUBLANES, :] = jnp.zeros((SUBLANES, D_MODEL), F32)

    x = _rms(h_ref[...], g_ref[...])
    x_buf[SUBLANES:SUBLANES + tb, :] = x
    xx = x_buf[SUBLANES - 1:SUBLANES - 1 + tb, :] - x
    x_buf[0:SUBLANES, :] = x_buf[tb:tb + SUBLANES, :]

    def mix(n):
        return x + xx * mu_ref[n:n + 1, :]

    r = _mm(mix(0), wr_ref[...])
    k = _mm(mix(2), wk_ref[...])
    xv = mix(3)
    v = _mm(xv, wv_ref[...])
    w = -jax.nn.softplus(-(w0_ref[...] + _mm(jnp.tanh(_mm(mix(1), w1_ref[...])), w2_ref[...]))) - 0.5
    a = jax.nn.sigmoid(a0_ref[...] + _mm(_mm(mix(4), a1_ref[...]), a2_ref[...]))
    g_o[...] = _mm(jax.nn.sigmoid(_mm(mix(5), g1_ref[...])), g2_ref[...])

    kk = k * kk_ref[...]
    ss = _mm_exact01(_tile_sum(kk * kk), mkk_ref[...])
    inv = 1.0 / jnp.maximum(jnp.sqrt(ss), 1e-12)
    kk = jnp.concatenate([t * inv for t in _lane_tiles(kk)], axis=1)
    k = k * (1.0 + (a - 1.0) * ka_ref[...])
    if has_vres:
        v = v + (vf_ref[...] - v) * jax.nn.sigmoid(v0_ref[...] + _mm(_mm(xv, v1_ref[...]), v2_ref[...]))

    r_o[...] = r
    w_o[...] = jnp.exp(-jnp.exp(w))
    k_o[...] = k
    a_o[...] = -kk
    b_o[...] = kk * a
    v_o[...] = v
    bon_o[...] = _mm_exact01(_tile_sum(r * k * rk_ref[...]), mkv_ref[...])


def _rwkv_pre(h, g, mu, wr, wk, wv, w0, w1, w2, a0, a1, a2, g1, g2, k_k, k_a, r_k, mkk, mkv, vres):
    t = h.shape[0]
    tb = TB_RW
    lw, la, lg = w1.shape[1], a1.shape[1], g1.shape[1]
    in_specs = [
        _row_spec(tb, D_MODEL), _const_spec((1, D_MODEL)), _const_spec((6, D_MODEL)),
        _const_spec((D_MODEL, D_MODEL)), _const_spec((D_MODEL, D_MODEL)), _const_spec((D_MODEL, D_MODEL)),
        _const_spec((1, D_MODEL)), _const_spec((D_MODEL, lw)), _const_spec((lw, D_MODEL)),
        _const_spec((1, D_MODEL)), _const_spec((D_MODEL, la)), _const_spec((la, D_MODEL)),
        _const_spec((D_MODEL, lg)), _const_spec((lg, D_MODEL)),
        _const_spec((1, D_MODEL)), _const_spec((1, D_MODEL)), _const_spec((1, D_MODEL)),
        _const_spec((LANES, LANES)), _const_spec((LANES, LANES)),
    ]
    args = [h, g, mu, wr, wk, wv, w0, w1, w2, a0, a1, a2, g1, g2, k_k, k_a, r_k, mkk, mkv]
    if vres is not None:
        v_first, v0, v1, v2 = vres
        lv = v1.shape[1]
        in_specs += [_row_spec(tb, D_MODEL), _const_spec((1, D_MODEL)),
                     _const_spec((D_MODEL, lv)), _const_spec((lv, D_MODEL))]
        args += [v_first, v0, v1, v2]
    wide = jax.ShapeDtypeStruct((t, D_MODEL), F32)
    return pl.pallas_call(
        functools.partial(_rwkv_pre_kernel, tb=tb, has_vres=vres is not None),
        grid=(t // tb,),
        in_specs=in_specs,
        out_specs=[_row_spec(tb, D_MODEL)] * 7 + [_row_spec(tb, LANES)],
        out_shape=[wide] * 7 + [jax.ShapeDtypeStruct((t, LANES), F32)],
        scratch_shapes=[pltpu.VMEM((tb + SUBLANES, D_MODEL), F32)],
        compiler_params=_cparams(),
        name="rwkv_pre",
    )(*args)


def _scan_kernel(w_ref, b_ref, k_ref, r_ref, a_ref, wn_ref, bn_ref, kn_ref, rn_ref, an_ref, v_ref, rm_ref, y_ref,
                 s_ref, u_ref, xbuf_ref, lhs_ref, e0_ref, e1_ref, *, tb):
    n_groups = tb // SUBLANES
    tiles_per_mxu = MXU_N // LANES

    for c, (cur, nxt) in enumerate(zip((w_ref, b_ref, k_ref, r_ref, a_ref), (wn_ref, bn_ref, kn_ref, rn_ref, an_ref))):
        for j in range(N_TILES):
            xbuf_ref[c * N_TILES + j, 0:tb, :] = cur[:, LANES * j:LANES * (j + 1)]
            xbuf_ref[c * N_TILES + j, tb:tb + 2 * SUBLANES, :] = nxt[:, LANES * j:LANES * (j + 1)]

    def build_lhs(g):
        r0 = pl.multiple_of(g * SUBLANES, SUBLANES)
        tiles = []
        for c in range(N_COEF):
            for j in range(N_TILES):
                start = r0 if c < N_COEF - 1 else g * SUBLANES + 1
                tiles.append(xbuf_ref[c * N_TILES + j, pl.ds(start, SUBLANES), :])
        lhs = jnp.concatenate(_split3(jnp.concatenate(tiles, axis=0)), axis=1)
        lhs_ref[0] = lhs[0:HALF_RB * SUBLANES]
        lhs_ref[1] = lhs[HALF_RB * SUBLANES:]

    def expand_piece(j, e_ref):
        nt = j % (N_TILES // tiles_per_mxu)
        half = j // (N_TILES // tiles_per_mxu)
        rows = HALF_RB * SUBLANES // 2
        out = jnp.concatenate([jnp.dot(lhs_ref[half, q * rows:(q + 1) * rows, :], rm_ref[nt],
                                       preferred_element_type=F32) for q in range(2)], axis=0)
        for rb in range(HALF_RB):
            q = half * HALF_RB + rb
            c, jt = q // N_TILES, q % N_TILES
            for kk in range(tiles_per_mxu):
                idx = (SUBLANES * jt + tiles_per_mxu * nt + kk) * N_COEF + c
                e_ref[idx] = out[SUBLANES * rb:SUBLANES * (rb + 1), LANES * kk:LANES * (kk + 1)]

    @pl.when(pl.program_id(0) == 0)
    def _():
        s_ref[...] = jnp.zeros_like(s_ref)
        u_ref[...] = jnp.zeros_like(u_ref)
        build_lhs(0)
        for j in range(SUBLANES):
            expand_piece(j, e0_ref)

    always = pl.program_id(0) >= 0

    def steps(t0, e_cur, e_nxt):
        def bc(k, c, j):
            return jnp.broadcast_to(e_cur[k * N_COEF + c, j:j + 1, :], (SUBLANES, LANES))

        for j in range(SUBLANES):
            @pl.when(always)
            def _(j=j):
                expand_piece(j, e_nxt)
                u = u_ref[...]
                val = v_ref[t0 + j]
                yp = [None] * N_ACC
                up = [None] * N_ACC
                for k in range(RW_HEAD):
                    sn = s_ref[k] * bc(k, 0, j) + u * bc(k, 1, j) + val * bc(k, 2, j)
                    s_ref[k] = sn
                    ty = sn * bc(k, 3, j)
                    tu = sn * bc(k, 4, j)
                    yp[k % N_ACC] = ty if yp[k % N_ACC] is None else yp[k % N_ACC] + ty
                    up[k % N_ACC] = tu if up[k % N_ACC] is None else up[k % N_ACC] + tu
                y_ref[t0 + j] = _tree_sum(yp)
                u_ref[...] = _tree_sum(up)

    def pair(ii, carry):
        g0 = 2 * ii
        build_lhs(g0 + 1)
        steps(g0 * SUBLANES, e0_ref, e1_ref)
        build_lhs(g0 + 2)
        steps((g0 + 1) * SUBLANES, e1_ref, e0_ref)
        return carry

    lax.fori_loop(0, n_groups // 2, pair, 0)


def _rwkv_scan(w, b, k, r, a, v3, rm):
    t = r.shape[0]
    tb = TB_SCAN
    head_rows = 2 * SUBLANES
    blocks_per_tb = tb // head_rows
    last_head = t // head_rows - 1
    kspec = _row_spec(tb, D_MODEL)
    nspec = pl.BlockSpec((head_rows, D_MODEL), lambda i: (jnp.minimum((i + 1) * blocks_per_tb, last_head), 0))
    vspec = pl.BlockSpec((tb, SUBLANES, LANES), lambda i: (i, 0, 0))
    e_scr = pltpu.VMEM((RW_HEAD * N_COEF, SUBLANES, LANES), F32)
    return pl.pallas_call(
        functools.partial(_scan_kernel, tb=tb),
        grid=(t // tb,),
        in_specs=[kspec] * N_COEF + [nspec] * N_COEF + [vspec, _const_spec(rm.shape)],
        out_specs=vspec,
        out_shape=jax.ShapeDtypeStruct((t, SUBLANES, LANES), F32),
        scratch_shapes=[pltpu.VMEM((RW_HEAD, SUBLANES, LANES), F32), pltpu.VMEM((SUBLANES, LANES), F32),
                        pltpu.VMEM((N_RB, tb + head_rows, LANES), F32),
                        pltpu.VMEM((2, HALF_RB * SUBLANES, N_SPLIT * LANES), BF16), e_scr, e_scr],
        compiler_params=_cparams(),
        name="rwkv_scan",
    )(w, b, k, r, a, w, b, k, r, a, v3, rm)


def _rwkv_post_kernel(h_ref, y_ref, v_ref, g_ref, bon_ref, lnw_ref, lnb_ref, mvv_ref, wo_ref, o_ref):
    y = y_ref[...]
    inv_n = 1.0 / RW_HEAD
    mean = _mm_exact01(_tile_sum(y), mvv_ref[...]) * inv_n
    cen = [t - mean for t in _lane_tiles(y)]
    sq = cen[0] * cen[0]
    for c in cen[1:]:
        sq = sq + c * c
    rstd = lax.rsqrt(_mm_exact01(sq, mvv_ref[...]) * inv_n + RW_GN_EPS)
    yn = jnp.concatenate([c * rstd for c in cen], axis=1) * lnw_ref[...] + lnb_ref[...]
    bon = bon_ref[...]
    bonus = jnp.concatenate([bon * t for t in _lane_tiles(v_ref[...])], axis=1)
    out = (yn + bonus) * g_ref[...]
    o_ref[...] = h_ref[...] + _mm(out, wo_ref[...])


def _rwkv_post(h, y, v, g, bon, ln_w, ln_b, mvv, w_o):
    t = h.shape[0]
    tb = TB_RW
    return pl.pallas_call(
        _rwkv_post_kernel,
        grid=(t // tb,),
        in_specs=[_row_spec(tb, D_MODEL)] * 4 + [_row_spec(tb, LANES), _const_spec((1, D_MODEL)),
                  _const_spec((1, D_MODEL)), _const_spec((LANES, LANES)), _const_spec((D_MODEL, D_MODEL))],
        out_specs=_row_spec(tb, D_MODEL),
        out_shape=jax.ShapeDtypeStruct((t, D_MODEL), F32),
        compiler_params=_cparams(),
        name="rwkv_post",
    )(h, y, v, g, bon, ln_w, ln_b, mvv, w_o)


def _k_perm():
    c = np.arange(D_MODEL)
    return (c % RW_HEADS) * RW_HEAD + (c // RW_HEADS)


def _v_perm():
    c = np.arange(D_MODEL)
    vhi, head, vlo = c // LANES, (c % LANES) // SUBLANES, c % SUBLANES
    return head * RW_HEAD + vhi * SUBLANES + vlo


def _segment_matrices():
    i = np.arange(LANES)[:, None]
    c = np.arange(LANES)[None, :]
    mkk = (i % RW_HEADS) == (c % RW_HEADS)
    mkv = (i % RW_HEADS) == (c // SUBLANES)
    mvv = (i // SUBLANES) == (c // SUBLANES)
    ce = np.arange(D_MODEL)[None, :]
    rm = ((i // RW_HEADS) == (ce // LANES)) & ((i % RW_HEADS) == ((ce % LANES) // SUBLANES))
    rm = np.concatenate([rm] * N_SPLIT, axis=0)
    rm = np.stack([rm[:, MXU_N * n:MXU_N * (n + 1)] for n in range(D_MODEL // MXU_N)])
    return tuple(jnp.asarray(m, BF16) for m in (mkk, mkv, mvv, rm))


def kernel(x, p, norm_mix, norm_ffn, norm_ple, norm_final, lru_w_in, lru_b_in, lru_conv_w, lru_conv_b, lru_w_gate_a, lru_b_gate_a, lru_w_gate_x, lru_b_gate_x, lru_lambda, lru_w_out, lru_b_out, rw_mu, rw_w_rkv, rw_w_o, rw_w0, rw_w1, rw_w2, rw_a0, rw_a1, rw_a2, rw_v0, rw_v1, rw_v2, rw_g1, rw_g2, rw_k_k, rw_k_a, rw_r_k, rw_ln_w, rw_ln_b, ffn_w_in, ffn_w_out, ple_w_proj, ple_w_gate):
    batch, seq, d = x.shape
    assert batch == 1 and d == D_MODEL and seq % TB_FFN == 0
    h = x.reshape(seq, d)
    p3 = p.reshape(DEPTH, seq, D_PLE)
    kp, vp = _k_perm(), _v_perm()
    mkk, mkv, mvv, rm = _segment_matrices()
    row = lambda a: a.reshape(1, -1)
    bf = lambda a: a.astype(BF16)
    g_final = row(norm_final)
    v_first = None
    for i in range(DEPTH):
        j = i // N_MIXERS
        g_mix = row(norm_mix[i])
        if i % N_MIXERS == 0:
            h = _hawk(h, g_mix, bf(lru_w_in[j]), row(lru_b_in[j]), lru_conv_w[j], row(lru_conv_b[j]),
                      bf(lru_w_gate_a[j]), row(lru_b_gate_a[j]), bf(lru_w_gate_x[j]), row(lru_b_gate_x[j]),
                      row(lru_lambda[j]), bf(lru_w_out[j]), row(lru_b_out[j]))
        else:
            vres = None if j == 0 else (v_first, row(rw_v0[j - 1][vp]), bf(rw_v1[j - 1]), bf(rw_v2[j - 1][:, vp]))
            r, w, k, a, b, v, g, bon = _rwkv_pre(
                h, g_mix, rw_mu[j], bf(rw_w_rkv[j, 0][:, kp]), bf(rw_w_rkv[j, 1][:, kp]), bf(rw_w_rkv[j, 2][:, vp]),
                row(rw_w0[j][kp]), bf(rw_w1[j]), bf(rw_w2[j][:, kp]),
                row(rw_a0[j][kp]), bf(rw_a1[j]), bf(rw_a2[j][:, kp]),
                bf(rw_g1[j]), bf(rw_g2[j][:, vp]),
                row(rw_k_k[j][kp]), row(rw_k_a[j][kp]), row(rw_r_k[j].reshape(-1)[kp]), mkk, mkv, vres)
            if j == 0:
                v_first = v
            y = _rwkv_scan(w, b, k, r, a, v.reshape(seq, SUBLANES, LANES), rm)
            h = _rwkv_post(h, y.reshape(seq, d), v, g, bon, row(rw_ln_w[j][vp]), row(rw_ln_b[j][vp]), mvv,
                           bf(rw_w_o[j][vp, :]))
        h = _ffn_ple(h, p3, i, row(norm_ffn[i]), row(norm_ple[i]), g_final, bf(ffn_w_in[i]), bf(ffn_w_out[i]),
                     bf(ple_w_proj[i]), bf(ple_w_gate[i]), final=(i == DEPTH - 1))
    return h.reshape(batch, seq, d)
```

```python
import functools

import numpy as np
import jax
import jax.numpy as jnp
from jax import lax
from jax.experimental import pallas as pl
from jax.experimental.pallas import tpu as pltpu

F32 = jnp.float32
BF16 = jnp.bfloat16

D_MODEL = 1024
DEPTH = 4
N_MIXERS = 2
D_RNN = 1280
LRU_BLOCKS = 10
LRU_BLOCK = D_RNN // LRU_BLOCKS
CONV_WIDTH = 4
LRU_C = 8.0
RW_HEAD = 64
RW_HEADS = D_MODEL // RW_HEAD
RW_GN_EPS = 64e-5
D_FF = 2816
D_PLE = 256
RMS_EPS = 1e-6

LANES = 128
SUBLANES = 8
MXU_N = 256
TB_FFN = 512
TB_HAWK = 256
TB_RW = 256
TB_SCAN = 128
VMEM_LIMIT = 56 * 1024 * 1024

N_TILES = D_MODEL // LANES
N_COEF = 4
N_SPLIT = 2
SPAN = 16
N_RB = N_COEF * N_TILES
N_EXP_TILES = D_MODEL // MXU_N
EXP_PARTS = 2
EXP_STEPS = (0, 1, 2, 3, 4, 5, 6, 7)
REGION_STEPS = 4
N_ACC = 4


def _cparams():
    return pltpu.CompilerParams(dimension_semantics=("arbitrary",), vmem_limit_bytes=VMEM_LIMIT)


def _const_spec(shape):
    nd = len(shape)
    return pl.BlockSpec(shape, lambda i: (0,) * nd, pipeline_mode=pl.Buffered(1))


def _row_spec(tb, width):
    return pl.BlockSpec((tb, width), lambda i: (i, 0))


def _rms(x, g):
    return x * lax.rsqrt(jnp.mean(x * x, axis=-1, keepdims=True) + RMS_EPS) * g


def _mm(x, w):
    return jnp.dot(x.astype(BF16), w, preferred_element_type=F32)


def _split3(x):
    hi = x.astype(BF16)
    r1 = x - hi.astype(F32)
    mid = r1.astype(BF16)
    lo = (r1 - mid.astype(F32)).astype(BF16)
    return hi, mid, lo


def _round_to_two_bf16(x):
    hi = x.astype(BF16).astype(F32)
    return hi + (x - hi).astype(BF16).astype(F32)


def _mm_exact01(x, m):
    hi, mid, lo = _split3(x)
    acc = jnp.dot(hi, m, preferred_element_type=F32)
    acc = acc + jnp.dot(mid, m, preferred_element_type=F32)
    return acc + jnp.dot(lo, m, preferred_element_type=F32)


def _lane_tiles(x):
    return [x[:, LANES * j:LANES * (j + 1)] for j in range(x.shape[1] // LANES)]


def _tile_sum(x):
    tiles = _lane_tiles(x)
    acc = tiles[0]
    for t in tiles[1:]:
        acc = acc + t
    return acc


def _tree_sum(parts):
    while len(parts) > 1:
        parts = [parts[i] + parts[i + 1] if i + 1 < len(parts) else parts[i] for i in range(0, len(parts), 2)]
    return parts[0]


def _ffn_ple_kernel(h_ref, p_ref, gf_ref, gp_ref, gl_ref, win_ref, wout_ref, wproj_ref, wgate_ref, o_ref, *, final):
    h = h_ref[...]
    gu = _mm(_rms(h, gf_ref[...]), win_ref[...])
    act = jax.nn.silu(gu[:, :D_FF]) * gu[:, D_FF:]
    h = h + _mm(act, wout_ref[...])
    gate = jax.nn.sigmoid(_mm(_rms(h, gp_ref[...]), wgate_ref[...]))
    h = h + _mm(p_ref[...], wproj_ref[...]) * gate
    if final:
        h = _rms(h, gl_ref[...])
    o_ref[...] = h


def _ffn_ple(h, p3, layer, g_ffn, g_ple, g_final, w_in, w_out, w_proj, w_gate, final):
    t = h.shape[0]
    tb = TB_FFN
    return pl.pallas_call(
        functools.partial(_ffn_ple_kernel, final=final),
        grid=(t // tb,),
        in_specs=[
            _row_spec(tb, D_MODEL),
            pl.BlockSpec((None, tb, D_PLE), lambda i: (layer, i, 0)),
            _const_spec((1, D_MODEL)), _const_spec((1, D_MODEL)), _const_spec((1, D_MODEL)),
            _const_spec((D_MODEL, 2 * D_FF)), _const_spec((D_FF, D_MODEL)),
            _const_spec((D_PLE, D_MODEL)), _const_spec((D_MODEL, D_MODEL)),
        ],
        out_specs=_row_spec(tb, D_MODEL),
        out_shape=jax.ShapeDtypeStruct((t, D_MODEL), F32),
        compiler_params=_cparams(),
        name="ffn_ple",
    )(h, p3, g_ffn, g_ple, g_final, w_in, w_out, w_proj, w_gate)


def _hawk_kernel(h_ref, g_ref, win_ref, bin_ref, cw_ref, cb_ref, wa_ref, ba_ref, wx_ref, bx_ref,
                 lam_ref, wout_ref, bout_ref, o_ref, rec_buf, state_ref, *, tb):
    @pl.when(pl.program_id(0) == 0)
    def _():
        rec_buf[0:SUBLANES, :] = jnp.zeros((SUBLANES, D_RNN), F32)
        state_ref[...] = jnp.zeros_like(state_ref)

    h = h_ref[...]
    gx = _mm(_rms(h, g_ref[...]), win_ref[...]) + bin_ref[...]
    gate = jax.nn.gelu(gx[:, :D_RNN])
    rec_buf[SUBLANES:SUBLANES + tb, :] = gx[:, D_RNN:]

    conv = cb_ref[...]
    for k in range(CONV_WIDTH):
        off = SUBLANES - (CONV_WIDTH - 1) + k
        conv = conv + rec_buf[off:off + tb, :] * cw_ref[k:k + 1, :]
    rec_buf[0:SUBLANES, :] = rec_buf[tb:tb + SUBLANES, :]

    xb = conv.astype(BF16)
    ra, rx = [], []
    for n in range(LRU_BLOCKS):
        xs = xb[:, LRU_BLOCK * n:LRU_BLOCK * (n + 1)]
        ra.append(jnp.dot(xs, wa_ref[n], preferred_element_type=F32))
        rx.append(jnp.dot(xs, wx_ref[n], preferred_element_type=F32))
    r = jax.nn.sigmoid(jnp.concatenate(ra, axis=1) + ba_ref[...])
    i = jax.nn.sigmoid(jnp.concatenate(rx, axis=1) + bx_ref[...])
    log_a = -LRU_C * r * jax.nn.softplus(-lam_ref[...])
    a = jnp.exp(log_a)
    th = jnp.tanh(log_a)
    b = jnp.sqrt(-2.0 * th / (1.0 - th)) * (i * conv)

    n_groups = tb // SUBLANES
    a = a.reshape(n_groups, SUBLANES, D_RNN)
    b = b.reshape(n_groups, SUBLANES, D_RNN)
    row = lax.broadcasted_iota(jnp.int32, (n_groups, SUBLANES, D_RNN), 1)
    s = 1
    while s < SUBLANES:
        keep = row >= s
        a_s = jnp.where(keep, pltpu.roll(a, s, 1), 1.0)
        b_s = jnp.where(keep, pltpu.roll(b, s, 1), 0.0)
        b = a * b_s + b
        a = a * a_s
        s *= 2
    carry = state_ref[...]
    groups = []
    for n in range(n_groups):
        hg = a[n] * carry + b[n]
        groups.append(hg)
        carry = hg[SUBLANES - 1:SUBLANES, :]
    state_ref[...] = carry
    hs = jnp.concatenate(groups, axis=0)

    o_ref[...] = h + _mm(hs * gate, wout_ref[...]) + bout_ref[...]


def _hawk(h, g, w_in, b_in, conv_w, conv_b, wa, ba, wx, bx, lam, w_out, b_out):
    t = h.shape[0]
    tb = TB_HAWK
    return pl.pallas_call(
        functools.partial(_hawk_kernel, tb=tb),
        grid=(t // tb,),
        in_specs=[
            _row_spec(tb, D_MODEL), _const_spec((1, D_MODEL)),
            _const_spec((D_MODEL, 2 * D_RNN)), _const_spec((1, 2 * D_RNN)),
            _const_spec((CONV_WIDTH, D_RNN)), _const_spec((1, D_RNN)),
            _const_spec((LRU_BLOCKS, LRU_BLOCK, LRU_BLOCK)), _const_spec((1, D_RNN)),
            _const_spec((LRU_BLOCKS, LRU_BLOCK, LRU_BLOCK)), _const_spec((1, D_RNN)),
            _const_spec((1, D_RNN)), _const_spec((D_RNN, D_MODEL)), _const_spec((1, D_MODEL)),
        ],
        out_specs=_row_spec(tb, D_MODEL),
        out_shape=jax.ShapeDtypeStruct((t, D_MODEL), F32),
        scratch_shapes=[pltpu.VMEM((tb + SUBLANES, D_RNN), F32), pltpu.VMEM((1, D_RNN), F32)],
        compiler_params=_cparams(),
        name="hawk",
    )(h, g, w_in, b_in, conv_w, conv_b, wa, ba, wx, bx, lam, w_out, b_out)


def _rwkv_pre_kernel(*refs, tb, has_vres):
    if has_vres:
        (h_ref, g_ref, mu_ref, wr_ref, wk_ref, wv_ref, w0_ref, w1_ref, w2_ref, a0_ref, a1_ref, a2_ref,
         g1_ref, g2_ref, kk_ref, ka_ref, rk_ref, mkk_ref, mkv_ref, vf_ref, v0_ref, v1_ref, v2_ref,
         b_o, k_o, r_o, a_o, v_o, g_o, bon_o, g1_o, g2_o, x_buf) = refs
    else:
        (h_ref, g_ref, mu_ref, wr_ref, wk_ref, wv_ref, w0_ref, w1_ref, w2_ref, a0_ref, a1_ref, a2_ref,
         g1_ref, g2_ref, kk_ref, ka_ref, rk_ref, mkk_ref, mkv_ref,
         b_o, k_o, r_o, a_o, v_o, g_o, bon_o, g1_o, g2_o, x_buf) = refs

    @pl.when(pl.program_id(0) == 0)
    def _():
        x_buf[0:SUBLANES, :] = jnp.zeros((SUBLANES, D_MODEL), F32)

    x = _rms(h_ref[...], g_ref[...])
    x_buf[SUBLANES:SUBLANES + tb, :] = x
    xx = x_buf[SUBLANES - 1:SUBLANES - 1 + tb, :] - x
    x_buf[0:SUBLANES, :] = x_buf[tb:tb + SUBLANES, :]

    def mix(n):
        return x + xx * mu_ref[n:n + 1, :]

    r = _mm(mix(0), wr_ref[...])
    k = _mm(mix(2), wk_ref[...])
    xv = mix(3)
    v = _mm(xv, wv_ref[...])
    w = -jax.nn.softplus(-(w0_ref[...] + _mm(jnp.tanh(_mm(mix(1), w1_ref[...])), w2_ref[...]))) - 0.5
    a = jax.nn.sigmoid(a0_ref[...] + _mm(_mm(mix(4), a1_ref[...]), a2_ref[...]))
    g_o[...] = _mm(jax.nn.sigmoid(_mm(mix(5), g1_ref[...])), g2_ref[...])

    kk = k * kk_ref[...]
    ss = _mm_exact01(_tile_sum(kk * kk), mkk_ref[...])
    inv = 1.0 / jnp.maximum(jnp.sqrt(ss), 1e-12)
    kk = jnp.concatenate([t * inv for t in _lane_tiles(kk)], axis=1)
    k = k * (1.0 + (a - 1.0) * ka_ref[...])
    if has_vres:
        v = v + (vf_ref[...] - v) * jax.nn.sigmoid(v0_ref[...] + _mm(_mm(xv, v1_ref[...]), v2_ref[...]))

    v_o[...] = v
    bon_o[...] = _mm_exact01(_tile_sum(r * k * rk_ref[...]), mkv_ref[...])

    n_spans = tb // SPAN
    lw = (-jnp.exp(w)).reshape(n_spans, SPAN, D_MODEL)
    row = lax.broadcasted_iota(jnp.int32, (n_spans, SPAN, D_MODEL), 1)
    c = lw
    s = 1
    while s < SPAN:
        c = c + jnp.where(row >= s, pltpu.roll(c, s, 1), 0.0)
        s *= 2
    grow = jnp.exp(c).reshape(tb, D_MODEL)
    shrink = jnp.exp(-c).reshape(tb, D_MODEL)
    grow_prev = jnp.exp(c - lw).reshape(tb, D_MODEL)
    r_o[...] = r * grow
    k_o[...] = k * shrink
    b_o[...] = (kk * a) * shrink
    a_o[...] = -kk * grow_prev
    total = jnp.exp(c[:, SPAN - 1, :])
    g1 = _round_to_two_bf16(total)
    g1_o[...] = g1
    g2_o[...] = total / g1


def _rwkv_pre(h, g, mu, wr, wk, wv, w0, w1, w2, a0, a1, a2, g1, g2, k_k, k_a, r_k, mkk, mkv, vres):
    t = h.shape[0]
    tb = TB_RW
    lw, la, lg = w1.shape[1], a1.shape[1], g1.shape[1]
    in_specs = [
        _row_spec(tb, D_MODEL), _const_spec((1, D_MODEL)), _const_spec((6, D_MODEL)),
        _const_spec((D_MODEL, D_MODEL)), _const_spec((D_MODEL, D_MODEL)), _const_spec((D_MODEL, D_MODEL)),
        _const_spec((1, D_MODEL)), _const_spec((D_MODEL, lw)), _const_spec((lw, D_MODEL)),
        _const_spec((1, D_MODEL)), _const_spec((D_MODEL, la)), _const_spec((la, D_MODEL)),
        _const_spec((D_MODEL, lg)), _const_spec((lg, D_MODEL)),
        _const_spec((1, D_MODEL)), _const_spec((1, D_MODEL)), _const_spec((1, D_MODEL)),
        _const_spec((LANES, LANES)), _const_spec((LANES, LANES)),
    ]
    args = [h, g, mu, wr, wk, wv, w0, w1, w2, a0, a1, a2, g1, g2, k_k, k_a, r_k, mkk, mkv]
    if vres is not None:
        v_first, v0, v1, v2 = vres
        lv = v1.shape[1]
        in_specs += [_row_spec(tb, D_MODEL), _const_spec((1, D_MODEL)),
                     _const_spec((D_MODEL, lv)), _const_spec((lv, D_MODEL))]
        args += [v_first, v0, v1, v2]
    wide = jax.ShapeDtypeStruct((t, D_MODEL), F32)
    per_span = jax.ShapeDtypeStruct((t // SPAN, D_MODEL), F32)
    return pl.pallas_call(
        functools.partial(_rwkv_pre_kernel, tb=tb, has_vres=vres is not None),
        grid=(t // tb,),
        in_specs=in_specs,
        out_specs=[_row_spec(tb, D_MODEL)] * 6 + [_row_spec(tb, LANES)] + [_row_spec(tb // SPAN, D_MODEL)] * 2,
        out_shape=[wide] * 6 + [jax.ShapeDtypeStruct((t, LANES), F32), per_span, per_span],
        scratch_shapes=[pltpu.VMEM((tb + SUBLANES, D_MODEL), F32)],
        compiler_params=_cparams(),
        name="rwkv_pre",
    )(*args)


def _scan_kernel(b_ref, k_ref, r_ref, a_ref, bn_ref, kn_ref, rn_ref, an_ref, g1_ref, g2_ref, v_ref, rm_ref, y_ref,
                 s_ref, u_ref, xbuf_ref, lhs_ref, e0_ref, e1_ref, eg_ref, *, tb):
    n_groups = tb // SUBLANES
    n_spans = tb // SPAN
    tiles_per_mxu = MXU_N // LANES

    for c, (cur, nxt) in enumerate(zip((b_ref, k_ref, r_ref, a_ref), (bn_ref, kn_ref, rn_ref, an_ref))):
        for j in range(N_TILES):
            xbuf_ref[c * N_TILES + j, 0:tb, :] = cur[:, LANES * j:LANES * (j + 1)]
            xbuf_ref[c * N_TILES + j, tb:tb + 2 * SUBLANES, :] = nxt[:, LANES * j:LANES * (j + 1)]

    def split2(x):
        hi = x.astype(BF16)
        return jnp.concatenate([hi, (x - hi.astype(F32)).astype(BF16)], axis=1)

    gt = [ref[:, LANES * j:LANES * (j + 1)] for ref in (g1_ref, g2_ref) for j in range(N_TILES)]
    rm_all = jnp.concatenate([rm_ref[n] for n in range(N_EXP_TILES)], axis=1)
    gout = jnp.dot(split2(jnp.concatenate(gt, axis=0)), rm_all, preferred_element_type=F32)
    for which in range(2):
        for j in range(N_TILES):
            base = (which * N_TILES + j) * n_spans
            for kl in range(SUBLANES):
                eg_ref[which * RW_HEAD + SUBLANES * j + kl] = gout[base:base + n_spans, LANES * kl:LANES * (kl + 1)]

    def build_lhs(g):
        r0 = pl.multiple_of(g * SUBLANES, SUBLANES)
        tiles = []
        for c in range(N_COEF):
            for j in range(N_TILES):
                start = r0 if c < N_COEF - 1 else g * SUBLANES + 1
                tiles.append(xbuf_ref[c * N_TILES + j, pl.ds(start, SUBLANES), :])
        lhs_ref[...] = split2(jnp.concatenate(tiles, axis=0))

    def expand_piece(piece, e_ref):
        nt, part = piece // EXP_PARTS, piece % EXP_PARTS
        rb_per_part = N_RB // EXP_PARTS
        rows = rb_per_part * SUBLANES
        out = jnp.dot(lhs_ref[part * rows:(part + 1) * rows, :], rm_ref[nt], preferred_element_type=F32)
        for rb in range(rb_per_part):
            q = part * rb_per_part + rb
            c, jt = q // N_TILES, q % N_TILES
            for kk in range(tiles_per_mxu):
                idx = (SUBLANES * jt + tiles_per_mxu * nt + kk) * N_COEF + c
                e_ref[idx] = out[SUBLANES * rb:SUBLANES * (rb + 1), LANES * kk:LANES * (kk + 1)]

    @pl.when(pl.program_id(0) == 0)
    def _():
        s_ref[...] = jnp.zeros_like(s_ref)
        u_ref[...] = jnp.zeros_like(u_ref)
        build_lhs(0)
        for piece in range(N_EXP_TILES * EXP_PARTS):
            expand_piece(piece, e0_ref)

    always = pl.program_id(0) >= 0

    def steps(t0, e_cur, e_nxt, span):
        def bc(k, c, j):
            return jnp.broadcast_to(e_cur[k * N_COEF + c, j:j + 1, :], (SUBLANES, LANES))

        def step(j):
            for piece in range(N_EXP_TILES * EXP_PARTS):
                if EXP_STEPS[piece] == j:
                    expand_piece(piece, e_nxt)
            u = u_ref[...]
            val = v_ref[t0 + j]
            yp = [None] * N_ACC
            up = [None] * N_ACC
            for k in range(RW_HEAD):
                sn = s_ref[k] + u * bc(k, 0, j) + val * bc(k, 1, j)
                ty = sn * bc(k, 2, j)
                if span is not None and j == SUBLANES - 1:
                    g1 = jnp.broadcast_to(eg_ref[k, pl.ds(span, 1), :], (SUBLANES, LANES))
                    g2 = jnp.broadcast_to(eg_ref[RW_HEAD + k, pl.ds(span, 1), :], (SUBLANES, LANES))
                    sn = sn * g1 * g2
                s_ref[k] = sn
                tu = sn * bc(k, 3, j)
                yp[k % N_ACC] = ty if yp[k % N_ACC] is None else yp[k % N_ACC] + ty
                up[k % N_ACC] = tu if up[k % N_ACC] is None else up[k % N_ACC] + tu
            y_ref[t0 + j] = _tree_sum(yp)
            u_ref[...] = _tree_sum(up)

        for j0 in range(0, SUBLANES, REGION_STEPS):
            @pl.when(always)
            def _(j0=j0):
                for j in range(j0, j0 + REGION_STEPS):
                    step(j)

    def pair(ii, carry):
        g0 = 2 * ii
        build_lhs(g0 + 1)
        steps(g0 * SUBLANES, e0_ref, e1_ref, None)
        build_lhs(g0 + 2)
        steps((g0 + 1) * SUBLANES, e1_ref, e0_ref, ii)
        return carry

    lax.fori_loop(0, n_groups // 2, pair, 0)


def _rwkv_scan(b, k, r, a, g1, g2, v3, rm):
    t = r.shape[0]
    tb = TB_SCAN
    assert SPAN == 2 * SUBLANES and tb // SPAN == SUBLANES
    head_rows = 2 * SUBLANES
    blocks_per_tb = tb // head_rows
    last_head = t // head_rows - 1
    kspec = _row_spec(tb, D_MODEL)
    nspec = pl.BlockSpec((head_rows, D_MODEL), lambda i: (jnp.minimum((i + 1) * blocks_per_tb, last_head), 0))
    gspec = _row_spec(tb // SPAN, D_MODEL)
    vspec = pl.BlockSpec((tb, SUBLANES, LANES), lambda i: (i, 0, 0))
    e_scr = pltpu.VMEM((RW_HEAD * N_COEF, SUBLANES, LANES), F32)
    return pl.pallas_call(
        functools.partial(_scan_kernel, tb=tb),
        grid=(t // tb,),
        in_specs=[kspec] * N_COEF + [nspec] * N_COEF + [gspec, gspec, vspec, _const_spec(rm.shape)],
        out_specs=vspec,
        out_shape=jax.ShapeDtypeStruct((t, SUBLANES, LANES), F32),
        scratch_shapes=[pltpu.VMEM((RW_HEAD, SUBLANES, LANES), F32), pltpu.VMEM((SUBLANES, LANES), F32),
                        pltpu.VMEM((N_RB, tb + head_rows, LANES), F32),
                        pltpu.VMEM((N_RB * SUBLANES, N_SPLIT * LANES), BF16), e_scr, e_scr,
                        pltpu.VMEM((2 * RW_HEAD, tb // SPAN, LANES), F32)],
        compiler_params=_cparams(),
        name="rwkv_scan",
    )(b, k, r, a, b, k, r, a, g1, g2, v3, rm)


def _rwkv_post_kernel(h_ref, y_ref, v_ref, g_ref, bon_ref, lnw_ref, lnb_ref, mvv_ref, wo_ref, o_ref):
    y = y_ref[...]
    inv_n = 1.0 / RW_HEAD
    mean = _mm_exact01(_tile_sum(y), mvv_ref[...]) * inv_n
    cen = [t - mean for t in _lane_tiles(y)]
    sq = cen[0] * cen[0]
    for c in cen[1:]:
        sq = sq + c * c
    rstd = lax.rsqrt(_mm_exact01(sq, mvv_ref[...]) * inv_n + RW_GN_EPS)
    yn = jnp.concatenate([c * rstd for c in cen], axis=1) * lnw_ref[...] + lnb_ref[...]
    bon = bon_ref[...]
    bonus = jnp.concatenate([bon * t for t in _lane_tiles(v_ref[...])], axis=1)
    out = (yn + bonus) * g_ref[...]
    o_ref[...] = h_ref[...] + _mm(out, wo_ref[...])


def _rwkv_post(h, y, v, g, bon, ln_w, ln_b, mvv, w_o):
    t = h.shape[0]
    tb = TB_RW
    return pl.pallas_call(
        _rwkv_post_kernel,
        grid=(t // tb,),
        in_specs=[_row_spec(tb, D_MODEL)] * 4 + [_row_spec(tb, LANES), _const_spec((1, D_MODEL)),
                  _const_spec((1, D_MODEL)), _const_spec((LANES, LANES)), _const_spec((D_MODEL, D_MODEL))],
        out_specs=_row_spec(tb, D_MODEL),
        out_shape=jax.ShapeDtypeStruct((t, D_MODEL), F32),
        compiler_params=_cparams(),
        name="rwkv_post",
    )(h, y, v, g, bon, ln_w, ln_b, mvv, w_o)


def _k_perm():
    c = np.arange(D_MODEL)
    return (c % RW_HEADS) * RW_HEAD + (c // RW_HEADS)


def _v_perm():
    c = np.arange(D_MODEL)
    vhi, head, vlo = c // LANES, (c % LANES) // SUBLANES, c % SUBLANES
    return head * RW_HEAD + vhi * SUBLANES + vlo


def _segment_matrices():
    i = np.arange(LANES)[:, None]
    c = np.arange(LANES)[None, :]
    mkk = (i % RW_HEADS) == (c % RW_HEADS)
    mkv = (i % RW_HEADS) == (c // SUBLANES)
    mvv = (i // SUBLANES) == (c // SUBLANES)
    ce = np.arange(D_MODEL)[None, :]
    rm = ((i // RW_HEADS) == (ce // LANES)) & ((i % RW_HEADS) == ((ce % LANES) // SUBLANES))
    rm = np.concatenate([rm] * N_SPLIT, axis=0)
    rm = np.stack([rm[:, MXU_N * n:MXU_N * (n + 1)] for n in range(D_MODEL // MXU_N)])
    return tuple(jnp.asarray(m, BF16) for m in (mkk, mkv, mvv, rm))


def kernel(x, p, norm_mix, norm_ffn, norm_ple, norm_final, lru_w_in, lru_b_in, lru_conv_w, lru_conv_b, lru_w_gate_a, lru_b_gate_a, lru_w_gate_x, lru_b_gate_x, lru_lambda, lru_w_out, lru_b_out, rw_mu, rw_w_rkv, rw_w_o, rw_w0, rw_w1, rw_w2, rw_a0, rw_a1, rw_a2, rw_v0, rw_v1, rw_v2, rw_g1, rw_g2, rw_k_k, rw_k_a, rw_r_k, rw_ln_w, rw_ln_b, ffn_w_in, ffn_w_out, ple_w_proj, ple_w_gate):
    batch, seq, d = x.shape
    assert batch == 1 and d == D_MODEL and seq % TB_FFN == 0
    h = x.reshape(seq, d)
    p3 = p.reshape(DEPTH, seq, D_PLE)
    kp, vp = _k_perm(), _v_perm()
    mkk, mkv, mvv, rm = _segment_matrices()
    row = lambda a: a.reshape(1, -1)
    bf = lambda a: a.astype(BF16)
    g_final = row(norm_final)
    v_first = None
    for i in range(DEPTH):
        j = i // N_MIXERS
        g_mix = row(norm_mix[i])
        if i % N_MIXERS == 0:
            h = _hawk(h, g_mix, bf(lru_w_in[j]), row(lru_b_in[j]), lru_conv_w[j], row(lru_conv_b[j]),
                      bf(lru_w_gate_a[j]), row(lru_b_gate_a[j]), bf(lru_w_gate_x[j]), row(lru_b_gate_x[j]),
                      row(lru_lambda[j]), bf(lru_w_out[j]), row(lru_b_out[j]))
        else:
            vres = None if j == 0 else (v_first, row(rw_v0[j - 1][vp]), bf(rw_v1[j - 1]), bf(rw_v2[j - 1][:, vp]))
            b, k, r, a, v, g, bon, g1, g2 = _rwkv_pre(
                h, g_mix, rw_mu[j], bf(rw_w_rkv[j, 0][:, kp]), bf(rw_w_rkv[j, 1][:, kp]), bf(rw_w_rkv[j, 2][:, vp]),
                row(rw_w0[j][kp]), bf(rw_w1[j]), bf(rw_w2[j][:, kp]),
                row(rw_a0[j][kp]), bf(rw_a1[j]), bf(rw_a2[j][:, kp]),
                bf(rw_g1[j]), bf(rw_g2[j][:, vp]),
                row(rw_k_k[j][kp]), row(rw_k_a[j][kp]), row(rw_r_k[j].reshape(-1)[kp]), mkk, mkv, vres)
            if j == 0:
                v_first = v
            y = _rwkv_scan(b, k, r, a, g1, g2, v.reshape(seq, SUBLANES, LANES), rm)
            h = _rwkv_post(h, y.reshape(seq, d), v, g, bon, row(rw_ln_w[j][vp]), row(rw_ln_b[j][vp]), mvv,
                           bf(rw_w_o[j][vp, :]))
        h = _ffn_ple(h, p3, i, row(norm_ffn[i]), row(norm_ple[i]), g_final, bf(ffn_w_in[i]), bf(ffn_w_out[i]),
                     bf(ple_w_proj[i]), bf(ple_w_gate[i]), final=(i == DEPTH - 1))
    return h.reshape(batch, seq, d)
```

```python
import functools

import numpy as np
import jax
import jax.numpy as jnp
from jax import lax
from jax.experimental import pallas as pl
from jax.experimental.pallas import tpu as pltpu

F32 = jnp.float32
BF16 = jnp.bfloat16

D_MODEL = 1024
DEPTH = 4
N_MIXERS = 2
D_RNN = 1280
LRU_BLOCKS = 10
LRU_BLOCK = D_RNN // LRU_BLOCKS
CONV_WIDTH = 4
LRU_C = 8.0
RW_HEAD = 64
RW_HEADS = D_MODEL // RW_HEAD
RW_GN_EPS = 64e-5
D_FF = 2816
D_PLE = 256
RMS_EPS = 1e-6

LANES = 128
SUBLANES = 8
MXU_N = 256
TB_FFN = 512
TB_HAWK = 256
TB_RW = 256
TB_SCAN = 128
VMEM_LIMIT = 56 * 1024 * 1024

N_TILES = D_MODEL // LANES
N_COEF = 4
N_SPLIT = 2
SPAN = 16
N_RB = N_COEF * N_TILES
EXP_COLS = MXU_N
N_EXP_TILES = D_MODEL // EXP_COLS
EXP_STEPS = (0, 2, 4, 6)
REGION_STEPS = 4
N_ACC = 4


def _cparams():
    return pltpu.CompilerParams(dimension_semantics=("arbitrary",), vmem_limit_bytes=VMEM_LIMIT)


def _const_spec(shape):
    nd = len(shape)
    return pl.BlockSpec(shape, lambda i: (0,) * nd, pipeline_mode=pl.Buffered(1))


def _layer_spec(arr, layer):
    shape = tuple(arr.shape[1:])
    nd = len(shape)
    return pl.BlockSpec((None,) + shape, lambda i: (layer,) + (0,) * nd, pipeline_mode=pl.Buffered(1))


def _row_spec(tb, width):
    return pl.BlockSpec((tb, width), lambda i: (i, 0))


def _rms(x, g):
    return x * lax.rsqrt(jnp.mean(x * x, axis=-1, keepdims=True) + RMS_EPS) * g


def _mm(x, w):
    return jnp.dot(x.astype(BF16), w, preferred_element_type=F32)


def _split3(x):
    hi = x.astype(BF16)
    r1 = x - hi.astype(F32)
    mid = r1.astype(BF16)
    lo = (r1 - mid.astype(F32)).astype(BF16)
    return hi, mid, lo


def _round_to_two_bf16(x):
    hi = x.astype(BF16).astype(F32)
    return hi + (x - hi).astype(BF16).astype(F32)


def _mm_exact01(x, m):
    hi, mid, lo = _split3(x)
    acc = jnp.dot(hi, m, preferred_element_type=F32)
    acc = acc + jnp.dot(mid, m, preferred_element_type=F32)
    return acc + jnp.dot(lo, m, preferred_element_type=F32)


def _lane_tiles(x):
    return [x[:, LANES * j:LANES * (j + 1)] for j in range(x.shape[1] // LANES)]


def _tile_sum(x):
    tiles = _lane_tiles(x)
    acc = tiles[0]
    for t in tiles[1:]:
        acc = acc + t
    return acc


def _tree_sum(parts):
    while len(parts) > 1:
        parts = [parts[i] + parts[i + 1] if i + 1 < len(parts) else parts[i] for i in range(0, len(parts), 2)]
    return parts[0]


def _rwkv_out(y, v, g, bon, ln_w, ln_b, mvv, w_o):
    inv_n = 1.0 / RW_HEAD
    mean = _mm_exact01(_tile_sum(y), mvv) * inv_n
    cen = [t - mean for t in _lane_tiles(y)]
    sq = cen[0] * cen[0]
    for c in cen[1:]:
        sq = sq + c * c
    rstd = lax.rsqrt(_mm_exact01(sq, mvv) * inv_n + RW_GN_EPS)
    yn = jnp.concatenate([c * rstd for c in cen], axis=1) * ln_w + ln_b
    bonus = jnp.concatenate([bon * t for t in _lane_tiles(v)], axis=1)
    return _mm((yn + bonus) * g, w_o)


def _ffn_ple_kernel(*refs, final, has_mix):
    if has_mix:
        (h_ref, y_ref, v_ref, g_ref, bon_ref, lnw_ref, lnb_ref, mvv_ref, wo_ref,
         p_ref, gf_ref, gp_ref, gl_ref, win_ref, wout_ref, wproj_ref, wgate_ref, o_ref) = refs
        h = h_ref[...] + _rwkv_out(y_ref[...], v_ref[...], g_ref[...], bon_ref[...], lnw_ref[...], lnb_ref[...],
                                   mvv_ref[...], wo_ref[...])
    else:
        h_ref, p_ref, gf_ref, gp_ref, gl_ref, win_ref, wout_ref, wproj_ref, wgate_ref, o_ref = refs
        h = h_ref[...]
    gu = _mm(_rms(h, gf_ref[...]), win_ref[...])
    act = jax.nn.silu(gu[:, :D_FF]) * gu[:, D_FF:]
    h = h + _mm(act, wout_ref[...])
    gate = jax.nn.sigmoid(_mm(_rms(h, gp_ref[...]), wgate_ref[...]))
    h = h + _mm(p_ref[...], wproj_ref[...]) * gate
    if final:
        h = _rms(h, gl_ref[...])
    o_ref[...] = h


def _ffn_ple(h, p3, layer, prm, mix=None):
    t = h.shape[0]
    tb = TB_FFN
    in_specs = [_row_spec(tb, D_MODEL)]
    args = [h]
    if mix is not None:
        y, v, g, bon, ln_w, ln_b, mvv, w_o, j = mix
        in_specs += [_row_spec(tb, D_MODEL)] * 3 + [_row_spec(tb, LANES), _layer_spec(ln_w, j), _layer_spec(ln_b, j),
                                                   _const_spec(mvv.shape), _layer_spec(w_o, j)]
        args += [y, v, g, bon, ln_w, ln_b, mvv, w_o]
    in_specs += [pl.BlockSpec((None, tb, D_PLE), lambda i: (layer, i, 0)),
                 _layer_spec(prm["g_ffn"], layer), _layer_spec(prm["g_ple"], layer), _const_spec(prm["g_final"].shape),
                 _layer_spec(prm["w_in"], layer), _layer_spec(prm["w_out"], layer),
                 _layer_spec(prm["w_proj"], layer), _layer_spec(prm["w_gate"], layer)]
    args += [p3, prm["g_ffn"], prm["g_ple"], prm["g_final"], prm["w_in"], prm["w_out"], prm["w_proj"], prm["w_gate"]]
    return pl.pallas_call(
        functools.partial(_ffn_ple_kernel, final=(layer == DEPTH - 1), has_mix=mix is not None),
        grid=(t // tb,),
        in_specs=in_specs,
        out_specs=_row_spec(tb, D_MODEL),
        out_shape=jax.ShapeDtypeStruct((t, D_MODEL), F32),
        compiler_params=_cparams(),
        name="ffn_ple",
    )(*args)


def _hawk_kernel(h_ref, g_ref, win_ref, bin_ref, cw_ref, cb_ref, wa_ref, ba_ref, wx_ref, bx_ref,
                 lam_ref, wout_ref, bout_ref, o_ref, rec_buf, state_ref, *, tb):
    @pl.when(pl.program_id(0) == 0)
    def _():
        rec_buf[0:SUBLANES, :] = jnp.zeros((SUBLANES, D_RNN), F32)
        state_ref[...] = jnp.zeros_like(state_ref)

    h = h_ref[...]
    gx = _mm(_rms(h, g_ref[...]), win_ref[...]) + bin_ref[...]
    gate = jax.nn.gelu(gx[:, :D_RNN])
    rec_buf[SUBLANES:SUBLANES + tb, :] = gx[:, D_RNN:]

    conv = cb_ref[...]
    for k in range(CONV_WIDTH):
        off = SUBLANES - (CONV_WIDTH - 1) + k
        conv = conv + rec_buf[off:off + tb, :] * cw_ref[k:k + 1, :]
    rec_buf[0:SUBLANES, :] = rec_buf[tb:tb + SUBLANES, :]

    xb = conv.astype(BF16)
    ra, rx = [], []
    for n in range(LRU_BLOCKS):
        xs = xb[:, LRU_BLOCK * n:LRU_BLOCK * (n + 1)]
        ra.append(jnp.dot(xs, wa_ref[n], preferred_element_type=F32))
        rx.append(jnp.dot(xs, wx_ref[n], preferred_element_type=F32))
    r = jax.nn.sigmoid(jnp.concatenate(ra, axis=1) + ba_ref[...])
    i = jax.nn.sigmoid(jnp.concatenate(rx, axis=1) + bx_ref[...])
    log_a = -LRU_C * r * jax.nn.softplus(-lam_ref[...])
    a = jnp.exp(log_a)
    th = jnp.tanh(log_a)
    b = jnp.sqrt(-2.0 * th / (1.0 - th)) * (i * conv)

    n_groups = tb // SUBLANES
    a = a.reshape(n_groups, SUBLANES, D_RNN)
    b = b.reshape(n_groups, SUBLANES, D_RNN)
    row = lax.broadcasted_iota(jnp.int32, (n_groups, SUBLANES, D_RNN), 1)
    s = 1
    while s < SUBLANES:
        keep = row >= s
        a_s = jnp.where(keep, pltpu.roll(a, s, 1), 1.0)
        b_s = jnp.where(keep, pltpu.roll(b, s, 1), 0.0)
        b = a * b_s + b
        a = a * a_s
        s *= 2
    carry = state_ref[...]
    groups = []
    for n in range(n_groups):
        hg = a[n] * carry + b[n]
        groups.append(hg)
        carry = hg[SUBLANES - 1:SUBLANES, :]
    state_ref[...] = carry
    hs = jnp.concatenate(groups, axis=0)

    o_ref[...] = h + _mm(hs * gate, wout_ref[...]) + bout_ref[...]


def _hawk(h, g_mix, layer, prm, j):
    t = h.shape[0]
    tb = TB_HAWK
    names = ("w_in", "b_in", "conv_w", "conv_b", "wa", "ba", "wx", "bx", "lam", "w_out", "b_out")
    return pl.pallas_call(
        functools.partial(_hawk_kernel, tb=tb),
        grid=(t // tb,),
        in_specs=[_row_spec(tb, D_MODEL), _layer_spec(g_mix, layer)] + [_layer_spec(prm[n], j) for n in names],
        out_specs=_row_spec(tb, D_MODEL),
        out_shape=jax.ShapeDtypeStruct((t, D_MODEL), F32),
        scratch_shapes=[pltpu.VMEM((tb + SUBLANES, D_RNN), F32), pltpu.VMEM((1, D_RNN), F32)],
        compiler_params=_cparams(),
        name="hawk",
    )(h, g_mix, *[prm[n] for n in names])


def _rwkv_pre_kernel(*refs, tb, has_vres):
    if has_vres:
        (h_ref, g_ref, mu_ref, wr_ref, wk_ref, wv_ref, w0_ref, w1_ref, w2_ref, a0_ref, a1_ref, a2_ref,
         g1_ref, g2_ref, kk_ref, ka_ref, rk_ref, mkk_ref, mkv_ref, vf_ref, v0_ref, v1_ref, v2_ref,
         b_o, k_o, r_o, a_o, v_o, g_o, bon_o, g1_o, g2_o, x_buf) = refs
    else:
        (h_ref, g_ref, mu_ref, wr_ref, wk_ref, wv_ref, w0_ref, w1_ref, w2_ref, a0_ref, a1_ref, a2_ref,
         g1_ref, g2_ref, kk_ref, ka_ref, rk_ref, mkk_ref, mkv_ref,
         b_o, k_o, r_o, a_o, v_o, g_o, bon_o, g1_o, g2_o, x_buf) = refs

    @pl.when(pl.program_id(0) == 0)
    def _():
        x_buf[0:SUBLANES, :] = jnp.zeros((SUBLANES, D_MODEL), F32)

    x = _rms(h_ref[...], g_ref[...])
    x_buf[SUBLANES:SUBLANES + tb, :] = x
    xx = x_buf[SUBLANES - 1:SUBLANES - 1 + tb, :] - x
    x_buf[0:SUBLANES, :] = x_buf[tb:tb + SUBLANES, :]

    def mix(n):
        return x + xx * mu_ref[n:n + 1, :]

    r = _mm(mix(0), wr_ref[...])
    k = _mm(mix(2), wk_ref[...])
    xv = mix(3)
    v = _mm(xv, wv_ref[...])
    w = -jax.nn.softplus(-(w0_ref[...] + _mm(jnp.tanh(_mm(mix(1), w1_ref[...])), w2_ref[...]))) - 0.5
    a = jax.nn.sigmoid(a0_ref[...] + _mm(_mm(mix(4), a1_ref[...]), a2_ref[...]))
    g_o[...] = _mm(jax.nn.sigmoid(_mm(mix(5), g1_ref[...])), g2_ref[...])

    kk = k * kk_ref[...]
    ss = _mm_exact01(_tile_sum(kk * kk), mkk_ref[...])
    inv = 1.0 / jnp.maximum(jnp.sqrt(ss), 1e-12)
    kk = jnp.concatenate([t * inv for t in _lane_tiles(kk)], axis=1)
    k = k * (1.0 + (a - 1.0) * ka_ref[...])
    if has_vres:
        v = v + (vf_ref[...] - v) * jax.nn.sigmoid(v0_ref[...] + _mm(_mm(xv, v1_ref[...]), v2_ref[...]))

    v_o[...] = v
    bon_o[...] = _mm_exact01(_tile_sum(r * k * rk_ref[...]), mkv_ref[...])

    n_spans = tb // SPAN
    lw = (-jnp.exp(w)).reshape(n_spans, SPAN, D_MODEL)
    row = lax.broadcasted_iota(jnp.int32, (n_spans, SPAN, D_MODEL), 1)
    c = lw
    s = 1
    while s < SPAN:
        c = c + jnp.where(row >= s, pltpu.roll(c, s, 1), 0.0)
        s *= 2
    grow = jnp.exp(c).reshape(tb, D_MODEL)
    shrink = jnp.exp(-c).reshape(tb, D_MODEL)
    grow_prev = jnp.exp(c - lw).reshape(tb, D_MODEL)
    r_o[...] = r * grow
    k_o[...] = k * shrink
    b_o[...] = (kk * a) * shrink
    a_o[...] = -kk * grow_prev
    total = jnp.exp(c[:, SPAN - 1, :])
    g1 = _round_to_two_bf16(total)
    g1_o[...] = g1
    g2_o[...] = total / g1


def _rwkv_pre(h, g_mix, layer, prm, j, mkk, mkv, v_first):
    t = h.shape[0]
    tb = TB_RW
    names = ("mu", "wr", "wk", "wv", "w0", "w1", "w2", "a0", "a1", "a2", "g1", "g2", "k_k", "k_a", "r_k")
    in_specs = ([_row_spec(tb, D_MODEL), _layer_spec(g_mix, layer)] + [_layer_spec(prm[n], j) for n in names]
                + [_const_spec(mkk.shape), _const_spec(mkv.shape)])
    args = [h, g_mix] + [prm[n] for n in names] + [mkk, mkv]
    if v_first is not None:
        in_specs += [_row_spec(tb, D_MODEL)] + [_layer_spec(prm[n], j - 1) for n in ("v0", "v1", "v2")]
        args += [v_first, prm["v0"], prm["v1"], prm["v2"]]
    wide = jax.ShapeDtypeStruct((t, D_MODEL), F32)
    per_span = jax.ShapeDtypeStruct((t // SPAN, D_MODEL), F32)
    return pl.pallas_call(
        functools.partial(_rwkv_pre_kernel, tb=tb, has_vres=v_first is not None),
        grid=(t // tb,),
        in_specs=in_specs,
        out_specs=[_row_spec(tb, D_MODEL)] * 6 + [_row_spec(tb, LANES)] + [_row_spec(tb // SPAN, D_MODEL)] * 2,
        out_shape=[wide] * 6 + [jax.ShapeDtypeStruct((t, LANES), F32), per_span, per_span],
        scratch_shapes=[pltpu.VMEM((tb + SUBLANES, D_MODEL), F32)],
        compiler_params=_cparams(),
        name="rwkv_pre",
    )(*args)


def _scan_kernel(b_ref, k_ref, r_ref, a_ref, bn_ref, kn_ref, rn_ref, an_ref, g1_ref, g2_ref, v_ref, rm_ref, y_ref,
                 s_ref, u_ref, xbuf_ref, lhs_ref, e0_ref, e1_ref, eg_ref, *, tb):
    n_groups = tb // SUBLANES
    n_spans = tb // SPAN
    tiles_per_dot = EXP_COLS // LANES

    for c, (cur, nxt) in enumerate(zip((b_ref, k_ref, r_ref, a_ref), (bn_ref, kn_ref, rn_ref, an_ref))):
        for j in range(N_TILES):
            xbuf_ref[c * N_TILES + j, 0:tb, :] = cur[:, LANES * j:LANES * (j + 1)]
            xbuf_ref[c * N_TILES + j, tb:tb + 2 * SUBLANES, :] = nxt[:, LANES * j:LANES * (j + 1)]

    def split2(x):
        hi = x.astype(BF16)
        return jnp.concatenate([hi, (x - hi.astype(F32)).astype(BF16)], axis=1)

    gt = [ref[:, LANES * j:LANES * (j + 1)] for ref in (g1_ref, g2_ref) for j in range(N_TILES)]
    rm_all = jnp.concatenate([rm_ref[n] for n in range(N_EXP_TILES)], axis=1)
    gout = jnp.dot(split2(jnp.concatenate(gt, axis=0)), rm_all, preferred_element_type=F32)
    for which in range(2):
        for j in range(N_TILES):
            base = (which * N_TILES + j) * n_spans
            for kl in range(SUBLANES):
                eg_ref[which * RW_HEAD + SUBLANES * j + kl] = gout[base:base + n_spans, LANES * kl:LANES * (kl + 1)]

    def build_lhs(g):
        r0 = pl.multiple_of(g * SUBLANES, SUBLANES)
        tiles = []
        for c in range(N_COEF):
            for j in range(N_TILES):
                start = r0 if c < N_COEF - 1 else g * SUBLANES + 1
                tiles.append(xbuf_ref[c * N_TILES + j, pl.ds(start, SUBLANES), :])
        lhs_ref[...] = split2(jnp.concatenate(tiles, axis=0))

    def expand_tile(nt, e_ref):
        out = jnp.dot(lhs_ref[...], rm_ref[nt], preferred_element_type=F32)
        for q in range(N_RB):
            c, jt = q // N_TILES, q % N_TILES
            for kk in range(tiles_per_dot):
                idx = (SUBLANES * jt + tiles_per_dot * nt + kk) * N_COEF + c
                e_ref[idx] = out[SUBLANES * q:SUBLANES * (q + 1), LANES * kk:LANES * (kk + 1)]

    @pl.when(pl.program_id(0) == 0)
    def _():
        s_ref[...] = jnp.zeros_like(s_ref)
        u_ref[...] = jnp.zeros_like(u_ref)
        build_lhs(0)
        for nt in range(N_EXP_TILES):
            expand_tile(nt, e0_ref)

    always = pl.program_id(0) >= 0

    def steps(t0, e_cur, e_nxt, span):
        def bc(k, c, j):
            return jnp.broadcast_to(e_cur[k * N_COEF + c, j:j + 1, :], (SUBLANES, LANES))

        def step(j):
            for nt in range(N_EXP_TILES):
                if EXP_STEPS[nt] == j:
                    expand_tile(nt, e_nxt)
            u = u_ref[...]
            val = v_ref[t0 + j]
            yp = [None] * N_ACC
            up = [None] * N_ACC
            for k in range(RW_HEAD):
                sn = s_ref[k] + u * bc(k, 0, j) + val * bc(k, 1, j)
                ty = sn * bc(k, 2, j)
                if span is not None and j == SUBLANES - 1:
                    g1 = jnp.broadcast_to(eg_ref[k, pl.ds(span, 1), :], (SUBLANES, LANES))
                    g2 = jnp.broadcast_to(eg_ref[RW_HEAD + k, pl.ds(span, 1), :], (SUBLANES, LANES))
                    sn = sn * g1 * g2
                s_ref[k] = sn
                tu = sn * bc(k, 3, j)
                yp[k % N_ACC] = ty if yp[k % N_ACC] is None else yp[k % N_ACC] + ty
                up[k % N_ACC] = tu if up[k % N_ACC] is None else up[k % N_ACC] + tu
            y_ref[t0 + j] = _tree_sum(yp)
            u_ref[...] = _tree_sum(up)

        for j0 in range(0, SUBLANES, REGION_STEPS):
            @pl.when(always)
            def _(j0=j0):
                for j in range(j0, j0 + REGION_STEPS):
                    step(j)

    def pair(ii, carry):
        g0 = 2 * ii
        build_lhs(g0 + 1)
        steps(g0 * SUBLANES, e0_ref, e1_ref, None)
        build_lhs(g0 + 2)
        steps((g0 + 1) * SUBLANES, e1_ref, e0_ref, ii)
        return carry

    lax.fori_loop(0, n_groups // 2, pair, 0)


def _rwkv_scan(b, k, r, a, g1, g2, v3, rm):
    t = r.shape[0]
    tb = TB_SCAN
    assert SPAN == 2 * SUBLANES and tb // SPAN == SUBLANES
    head_rows = 2 * SUBLANES
    blocks_per_tb = tb // head_rows
    last_head = t // head_rows - 1
    kspec = _row_spec(tb, D_MODEL)
    nspec = pl.BlockSpec((head_rows, D_MODEL), lambda i: (jnp.minimum((i + 1) * blocks_per_tb, last_head), 0))
    gspec = _row_spec(tb // SPAN, D_MODEL)
    vspec = pl.BlockSpec((tb, SUBLANES, LANES), lambda i: (i, 0, 0))
    e_scr = pltpu.VMEM((RW_HEAD * N_COEF, SUBLANES, LANES), F32)
    return pl.pallas_call(
        functools.partial(_scan_kernel, tb=tb),
        grid=(t // tb,),
        in_specs=[kspec] * N_COEF + [nspec] * N_COEF + [gspec, gspec, vspec, _const_spec(rm.shape)],
        out_specs=vspec,
        out_shape=jax.ShapeDtypeStruct((t, SUBLANES, LANES), F32),
        scratch_shapes=[pltpu.VMEM((RW_HEAD, SUBLANES, LANES), F32), pltpu.VMEM((SUBLANES, LANES), F32),
                        pltpu.VMEM((N_RB, tb + head_rows, LANES), F32),
                        pltpu.VMEM((N_RB * SUBLANES, N_SPLIT * LANES), BF16), e_scr, e_scr,
                        pltpu.VMEM((2 * RW_HEAD, tb // SPAN, LANES), F32)],
        compiler_params=_cparams(),
        name="rwkv_scan",
    )(b, k, r, a, b, k, r, a, g1, g2, v3, rm)


def _to_k_layout(a):
    return a.reshape(a.shape[:-1] + (RW_HEADS, RW_HEAD)).swapaxes(-1, -2).reshape(a.shape)


def _to_v_layout(a, axis=-1):
    a = jnp.moveaxis(a, axis, -1)
    shape = a.shape
    a = a.reshape(shape[:-1] + (RW_HEADS, RW_HEAD // SUBLANES, SUBLANES)).swapaxes(-3, -2).reshape(shape)
    return jnp.moveaxis(a, -1, axis)


def _segment_matrices():
    i = np.arange(LANES)[:, None]
    c = np.arange(LANES)[None, :]
    mkk = (i % RW_HEADS) == (c % RW_HEADS)
    mkv = (i % RW_HEADS) == (c // SUBLANES)
    mvv = (i // SUBLANES) == (c // SUBLANES)
    ce = np.arange(D_MODEL)[None, :]
    rm = ((i // RW_HEADS) == (ce // LANES)) & ((i % RW_HEADS) == ((ce % LANES) // SUBLANES))
    rm = np.concatenate([rm] * N_SPLIT, axis=0)
    rm = np.stack([rm[:, EXP_COLS * n:EXP_COLS * (n + 1)] for n in range(N_EXP_TILES)])
    return tuple(jnp.asarray(m, BF16) for m in (mkk, mkv, mvv, rm))


def kernel(x, p, norm_mix, norm_ffn, norm_ple, norm_final, lru_w_in, lru_b_in, lru_conv_w, lru_conv_b, lru_w_gate_a, lru_b_gate_a, lru_w_gate_x, lru_b_gate_x, lru_lambda, lru_w_out, lru_b_out, rw_mu, rw_w_rkv, rw_w_o, rw_w0, rw_w1, rw_w2, rw_a0, rw_a1, rw_a2, rw_v0, rw_v1, rw_v2, rw_g1, rw_g2, rw_k_k, rw_k_a, rw_r_k, rw_ln_w, rw_ln_b, ffn_w_in, ffn_w_out, ple_w_proj, ple_w_gate):
    batch, seq, d = x.shape
    assert batch == 1 and d == D_MODEL and seq % TB_FFN == 0
    h = x.reshape(seq, d)
    p3 = p.reshape(DEPTH, seq, D_PLE)
    mkk, mkv, mvv, rm = _segment_matrices()
    vec = lambda a: a.reshape(a.shape[0], 1, -1)
    bf = lambda a: a.astype(BF16)
    kl, vl = _to_k_layout, _to_v_layout

    g_mix = vec(norm_mix)
    ffn = dict(g_ffn=vec(norm_ffn), g_ple=vec(norm_ple), g_final=norm_final.reshape(1, d), w_in=bf(ffn_w_in),
               w_out=bf(ffn_w_out), w_proj=bf(ple_w_proj), w_gate=bf(ple_w_gate))
    lru = dict(w_in=bf(lru_w_in), b_in=vec(lru_b_in), conv_w=lru_conv_w, conv_b=vec(lru_conv_b),
               wa=bf(lru_w_gate_a), ba=vec(lru_b_gate_a), wx=bf(lru_w_gate_x), bx=vec(lru_b_gate_x),
               lam=vec(lru_lambda), w_out=bf(lru_w_out), b_out=vec(lru_b_out))
    rw = dict(mu=rw_mu, wr=bf(kl(rw_w_rkv[:, 0])), wk=bf(kl(rw_w_rkv[:, 1])), wv=bf(vl(rw_w_rkv[:, 2])),
              w0=vec(kl(rw_w0)), w1=bf(rw_w1), w2=bf(kl(rw_w2)), a0=vec(kl(rw_a0)), a1=bf(rw_a1), a2=bf(kl(rw_a2)),
              g1=bf(rw_g1), g2=bf(vl(rw_g2)), k_k=vec(kl(rw_k_k)), k_a=vec(kl(rw_k_a)),
              r_k=vec(kl(rw_r_k.reshape(rw_r_k.shape[0], d))),
              v0=vec(vl(rw_v0)), v1=bf(rw_v1), v2=bf(vl(rw_v2)))
    ln_w, ln_b, w_o = vec(vl(rw_ln_w)), vec(vl(rw_ln_b)), bf(vl(rw_w_o, axis=-2))

    v_first = None
    for i in range(DEPTH):
        j = i // N_MIXERS
        if i % N_MIXERS == 0:
            h = _hawk(h, g_mix, i, lru, j)
            h = _ffn_ple(h, p3, i, ffn)
        else:
            b, k, r, a, v, g, bon, g1, g2 = _rwkv_pre(h, g_mix, i, rw, j, mkk, mkv, v_first)
            if j == 0:
                v_first = v
            y = _rwkv_scan(b, k, r, a, g1, g2, v.reshape(seq, SUBLANES, LANES), rm)
            h = _ffn_ple(h, p3, i, ffn, mix=(y.reshape(seq, d), v, g, bon, ln_w, ln_b, mvv, w_o, j))
    return h.reshape(batch, seq, d)
```
